```python
import jax, jax.numpy as jnp
from jax import lax
import numpy as np

D_MODEL = 2048
BATCH = 2
SEQ = 4096
DEPTH = 4

CHUNK = 64
D_CONV = D_MODEL // 2
D_ATTN = D_MODEL // 2
N_HEADS = 16
HEAD_DIM = D_ATTN // N_HEADS
CONV_K = 31
Q_BLOCK = 128
D_FF = 5632
N_EXPERTS = 8
TOP_K = 2
D_FF_EXPERT = D_FF // TOP_K
N_IN = 3 * D_ATTN + N_HEADS + 2 * D_CONV + 2 * D_MODEL
N_DENSE = (DEPTH + 1) // 2
N_MOE = DEPTH // 2
EPS = 1e-6
NEG_INF = -1e30

kernel_name = "hybrid_conformer_fox_moe_trunk"


def rmsnorm(x, g):
    xf = x.astype(jnp.float32)
    y = xf * lax.rsqrt(jnp.mean(xf * xf, axis=-1, keepdims=True) + EPS)
    return (y * g.astype(jnp.float32)).astype(x.dtype)


def layernorm(x, g, b):
    xf = x.astype(jnp.float32)
    mu = jnp.mean(xf, axis=-1, keepdims=True)
    xc = xf - mu
    y = xc * lax.rsqrt(jnp.mean(xc * xc, axis=-1, keepdims=True) + EPS)
    return (y * g.astype(jnp.float32) + b.astype(jnp.float32)).astype(x.dtype)


def modulate(h, shift, scale):
    return h * (1.0 + scale[:, None, :]) + shift[:, None, :]


def swiglu(t, w1, w3, w2):
    return (jax.nn.silu(t @ w1) * (t @ w3)) @ w2


def conformer_conv(u, conv_w, conv_b, ln_g, ln_b, w_proj):
    a, g = jnp.split(u, 2, axis=-1)
    y = a * jax.nn.sigmoid(g)
    y = lax.conv_general_dilated(
        y, conv_w[:, None, :], window_strides=(1,), padding=[(CONV_K - 1, 0)],
        dimension_numbers=("NWC", "WIO", "NWC"), feature_group_count=D_CONV) + conv_b
    y = jax.nn.silu(layernorm(y, ln_g, ln_b))
    return y @ w_proj


def forgetting_attention(q, k, v, fg_logit, b_f, w_o):
    B, S = q.shape[0], q.shape[1]
    def heads(t):
        return t.reshape(B, S, N_HEADS, HEAD_DIM).transpose(0, 2, 1, 3)
    q, k, v = heads(q), heads(k), heads(v)
    logf = jax.nn.log_sigmoid(fg_logit.astype(jnp.float32) + b_f.astype(jnp.float32))
    F = jnp.cumsum(logf, axis=1).transpose(0, 2, 1)
    nb = S // Q_BLOCK
    qb = q.reshape(B, N_HEADS, nb, Q_BLOCK, HEAD_DIM).transpose(2, 0, 1, 3, 4)
    Fqb = F.reshape(B, N_HEADS, nb, Q_BLOCK).transpose(2, 0, 1, 3)
    qpos = jnp.arange(S, dtype=jnp.int32).reshape(nb, Q_BLOCK)
    kpos = jnp.arange(S, dtype=jnp.int32)
    scale = HEAD_DIM ** -0.5

    def block(args):
        qi, Fqi, pi = args
        s = jnp.einsum("bhqd,bhkd->bhqk", qi, k).astype(jnp.float32) * scale
        s = s + (Fqi[..., :, None] - F[..., None, :])
        s = jnp.where(kpos[None, :] <= pi[:, None], s, NEG_INF)
        p = jax.nn.softmax(s, axis=-1).astype(v.dtype)
        return jnp.einsum("bhqk,bhkd->bhqd", p, v)

    o = lax.map(block, (qb, Fqb, qpos))
    o = o.transpose(1, 0, 3, 2, 4).reshape(B, S, D_ATTN)
    return o @ w_o


def moe_swiglu(h, router_w, router_b, w1, w3, w2):
    B, S, D = h.shape
    t = h.reshape(B * S, D)
    logits = (t @ router_w).astype(jnp.float32) + router_b.astype(jnp.float32)
    top_v, top_i = lax.top_k(logits, TOP_K)
    wts = jax.nn.softmax(top_v, axis=-1)
    gates = jnp.sum(jax.nn.one_hot(top_i, N_EXPERTS, dtype=jnp.float32) * wts[..., None], axis=1)
    y = jnp.zeros_like(t)
    for e in range(N_EXPERTS):
        y = y + gates[:, e:e + 1].astype(t.dtype) * swiglu(t, w1[e], w3[e], w2[e])
    return y.reshape(B, S, D)


def setup_inputs(seed: int = 0) -> dict:
    key = jax.random.key(seed)
    ks = jax.random.split(key, 32)
    f32 = jnp.float32
    def nrm(k, shape, s):
        return jax.random.normal(k, shape, f32) * s
    D = D_MODEL
    b_f = (jnp.linspace(1.0, 6.0, N_HEADS, dtype=f32)[None, :]
           + nrm(ks[8], (DEPTH, N_HEADS), 0.1))
    return {
        "x": nrm(ks[0], (BATCH, SEQ, D), 1.0),
        "c": nrm(ks[1], (BATCH, D), 1.0),
        "ada_w": nrm(ks[2], (DEPTH, D, 6 * D), 0.5 * D ** -0.5),
        "ada_b": nrm(ks[3], (DEPTH, 6 * D), 0.02),
        "norm_mix_g": 1.0 + nrm(ks[4], (DEPTH, D), 0.05),
        "norm_ffn_g": 1.0 + nrm(ks[5], (DEPTH, D), 0.05),
        "w_in": nrm(ks[6], (DEPTH, D, N_IN), D ** -0.5),
        "b_f": b_f,
        "conv_w": nrm(ks[9], (DEPTH, CONV_K, D_CONV), CONV_K ** -0.5),
        "conv_b": nrm(ks[10], (DEPTH, D_CONV), 0.02),
        "conv_ln_g": 1.0 + nrm(ks[11], (DEPTH, D_CONV), 0.05),
        "conv_ln_b": nrm(ks[12], (DEPTH, D_CONV), 0.02),
        "conv_out": nrm(ks[13], (DEPTH, D_CONV, D), D_CONV ** -0.5),
        "attn_out": nrm(ks[14], (DEPTH, D_ATTN, D), D_ATTN ** -0.5),
        "w_out": nrm(ks[15], (DEPTH, D, D), D ** -0.5),
        "ffn_w1": nrm(ks[16], (N_DENSE, D, D_FF), D ** -0.5),
        "ffn_w3": nrm(ks[17], (N_DENSE, D, D_FF), D ** -0.5),
        "ffn_w2": nrm(ks[18], (N_DENSE, D_FF, D), D_FF ** -0.5),
        "router_w": nrm(ks[19], (N_MOE, D, N_EXPERTS), D ** -0.5),
        "router_b": nrm(ks[20], (N_MOE, N_EXPERTS), 0.01),
        "moe_w1": nrm(ks[21], (N_MOE, N_EXPERTS, D, D_FF_EXPERT), D ** -0.5),
        "moe_w3": nrm(ks[22], (N_MOE, N_EXPERTS, D, D_FF_EXPERT), D ** -0.5),
        "moe_w2": nrm(ks[23], (N_MOE, N_EXPERTS, D_FF_EXPERT, D), D_FF_EXPERT ** -0.5),
        "final_norm_g": 1.0 + nrm(ks[24], (D,), 0.05),
    }


def reference(x, c, ada_w, ada_b, norm_mix_g, norm_ffn_g, w_in, b_f, conv_w, conv_b,
              conv_ln_g, conv_ln_b, conv_out, attn_out, w_out, ffn_w1, ffn_w3, ffn_w2,
              router_w, router_b, moe_w1, moe_w3, moe_w2, final_norm_g):
    splits = list(np.cumsum([D_ATTN, D_ATTN, D_ATTN, N_HEADS, 2 * D_CONV, D_MODEL]))
    c_act = jax.nn.silu(c)
    for l in range(DEPTH):
        mod = c_act @ ada_w[l] + ada_b[l]
        sh_m, sc_m, g_m, sh_f, sc_f, g_f = jnp.split(mod, 6, axis=-1)

        h = modulate(rmsnorm(x, norm_mix_g[l]), sh_m, sc_m)
        proj = h @ w_in[l]
        q, k, v, fg, u, ga, gb = jnp.split(proj, splits, axis=-1)
        y_a = conformer_conv(u, conv_w[l], conv_b[l], conv_ln_g[l], conv_ln_b[l], conv_out[l])
        y_b = forgetting_attention(q, k, v, fg, b_f[l], attn_out[l])
        merged = jax.nn.sigmoid(ga) * y_a + jax.nn.sigmoid(gb) * y_b
        x = x + g_m[:, None, :] * (merged @ w_out[l])

        h = modulate(rmsnorm(x, norm_ffn_g[l]), sh_f, sc_f)
        if l % 2 == 0:
            i = l // 2
            y = swiglu(h, ffn_w1[i], ffn_w3[i], ffn_w2[i])
        else:
            i = l // 2
            y = moe_swiglu(h, router_w[i], router_b[i], moe_w1[i], moe_w3[i], moe_w2[i])
        x = x + g_f[:, None, :] * y
    return rmsnorm(x, final_norm_g)
```

```python
import functools

import jax
import jax.numpy as jnp
from jax import lax
from jax.experimental import pallas as pl
from jax.experimental.pallas import tpu as pltpu

EPS = 1e-6
NEG_INF = -1e30
N_HEADS = 16
HEAD_DIM = 64
CONV_K = 31
N_EXPERTS = 8
TOP_K = 2

LANES = 128
CONV_HALO = 32
CONV_ROWS = 64
MIB = 1024 * 1024

F32 = jnp.float32
BF16 = jnp.bfloat16


def _params(semantics, vmem_mib):
    return pltpu.CompilerParams(dimension_semantics=semantics, vmem_limit_bytes=vmem_mib * MIB)


def _dot(a, b):
    return jnp.dot(a, b, preferred_element_type=F32)


def _silu(x):
    return x * jax.nn.sigmoid(x)


def _norm_mod(x, g, shift, scale):
    y = x * lax.rsqrt(jnp.mean(x * x, axis=-1, keepdims=True) + EPS) * g
    return y * (1.0 + scale) + shift


def _adaln_kernel(ct_ref, w_ref, b_ref, o_ref):
    w = w_ref[...]
    for b in range(o_ref.shape[0]):
        col = _silu(ct_ref[:, b:b + 1])
        o_ref[b:b + 1, :] = jnp.sum(w * col, axis=0, keepdims=True) + b_ref[...]


def _adaln(c, ada_w, ada_b, tn=1024):
    depth, d, n = ada_w.shape
    bsz = c.shape[0]
    return pl.pallas_call(
        _adaln_kernel,
        grid=(depth, n // tn),
        in_specs=[pl.BlockSpec((d, bsz), lambda l, j: (0, 0)),
                  pl.BlockSpec((None, d, tn), lambda l, j: (l, 0, j)),
                  pl.BlockSpec((None, 1, tn), lambda l, j: (l, 0, j))],
        out_specs=pl.BlockSpec((None, bsz, tn), lambda l, j: (l, 0, j)),
        out_shape=jax.ShapeDtypeStruct((depth, bsz, n), F32),
        compiler_params=_params(("arbitrary", "arbitrary"), 40),
        name="adaln",
    )(c.T, ada_w, ada_b.reshape(depth, 1, n))


def _mix_in_kernel(x_ref, mod_ref, g_ref, w_ref, wfg_ref, qkv_ref, glu_ref, gate_ref, fg_ref, h_ref,
                   *, n_q, n_qkv, n_glu):
    j = pl.program_id(1)

    @pl.when(j == 0)
    def _():
        h = _norm_mod(x_ref[...], g_ref[...], mod_ref[0:1, :], mod_ref[1:2, :]).astype(BF16)
        h_ref[...] = h
        fg_ref[...] = _dot(h, wfg_ref[...])

    acc = _dot(h_ref[...], w_ref[...])

    @pl.when(j < n_q)
    def _():
        qkv_ref[...] = (acc * HEAD_DIM ** -0.5).astype(BF16)

    @pl.when((j >= n_q) & (j < n_qkv))
    def _():
        qkv_ref[...] = acc.astype(BF16)

    @pl.when((j >= n_qkv) & (j < n_qkv + n_glu))
    def _():
        half = acc.shape[1] // 2
        glu_ref[...] = (acc[:, :half] * jax.nn.sigmoid(acc[:, half:])).astype(BF16)

    @pl.when(j >= n_qkv + n_glu)
    def _():
        gate_ref[...] = jax.nn.sigmoid(acc).astype(BF16)


def _mix_in(x, mod, g, w_all, w_fg, *, seq, d_attn, d_conv, tm=1024, tn=512):
    t, d = x.shape
    n_q, n_qkv, n_glu = d_attn // tn, 3 * d_attn // tn, 2 * d_conv // tn
    n_gate = 2 * d // tn
    n_tiles = n_qkv + n_glu + n_gate
    assert w_all.shape == (d, n_tiles * tn)
    tiles_per_batch = seq // tm
    kern = functools.partial(_mix_in_kernel, n_q=n_q, n_qkv=n_qkv, n_glu=n_glu)
    return pl.pallas_call(
        kern,
        grid=(t // tm, n_tiles),
        in_specs=[pl.BlockSpec((tm, d), lambda i, j: (i, 0)),
                  pl.BlockSpec((None, 6, d), lambda i, j: (i // tiles_per_batch, 0, 0)),
                  pl.BlockSpec((1, d), lambda i, j: (0, 0)),
                  pl.BlockSpec((d, tn), lambda i, j: (0, j)),
                  pl.BlockSpec((d, LANES), lambda i, j: (0, 0))],
        out_specs=[pl.BlockSpec((tm, tn), lambda i, j: (i, jnp.minimum(j, n_qkv - 1))),
                   pl.BlockSpec((tm, tn // 2), lambda i, j: (i, jnp.clip(j - n_qkv, 0, n_glu - 1))),
                   pl.BlockSpec((tm, tn), lambda i, j: (i, jnp.clip(j - n_qkv - n_glu, 0, n_gate - 1))),
                   pl.BlockSpec((tm, LANES), lambda i, j: (i, 0))],
        out_shape=[jax.ShapeDtypeStruct((t, 3 * d_attn), BF16),
                   jax.ShapeDtypeStruct((t, d_conv), BF16),
                   jax.ShapeDtypeStruct((t, 2 * d), BF16),
                   jax.ShapeDtypeStruct((t, LANES), F32)],
        scratch_shapes=[pltpu.VMEM((tm, d), BF16)],
        compiler_params=_params(("arbitrary", "arbitrary"), 48),
        name="mix_in",
    )(x, mod, g, w_all, w_fg)


def _arrange_w_in(w, *, d_attn, d_conv, tn=512):
    d = w.shape[0]
    w = w.astype(BF16)
    o = 3 * d_attn
    w_qkv = w[:, :o]
    w_fg = jnp.pad(w[:, o:o + N_HEADS], ((0, 0), (0, LANES - N_HEADS)))
    o += N_HEADS
    half = tn // 2
    u_a = w[:, o:o + d_conv].reshape(d, d_conv // half, 1, half)
    u_g = w[:, o + d_conv:o + 2 * d_conv].reshape(d, d_conv // half, 1, half)
    w_glu = jnp.concatenate([u_a, u_g], axis=2).reshape(d, 2 * d_conv)
    o += 2 * d_conv
    w_gates = w[:, o:]
    return jnp.concatenate([w_qkv, w_glu, w_gates], axis=1), w_fg


def _fox_gate_kernel(fg_ref, bf_ref, o_ref, carry_ref):
    @pl.when(pl.program_id(1) == 0)
    def _():
        carry_ref[...] = jnp.zeros_like(carry_ref)

    x = fg_ref[...] + bf_ref[...]
    logf = jnp.minimum(x, 0.0) - jnp.log(1.0 + jnp.exp(-jnp.abs(x)))
    n = x.shape[0]
    tri = (lax.broadcasted_iota(jnp.int32, (n, n), 1) <= lax.broadcasted_iota(jnp.int32, (n, n), 0)).astype(BF16)
    hi = logf.astype(BF16)
    r1 = logf - hi.astype(F32)
    mid = r1.astype(BF16)
    lo = (r1 - mid.astype(F32)).astype(BF16)
    cs = _dot(tri, hi) + _dot(tri, mid) + _dot(tri, lo) + carry_ref[...]
    o_ref[...] = cs
    carry_ref[...] = cs[n - 1:n, :]


def _fox_gate(fg, b_f, *, seq, chunk=256):
    t = fg.shape[0]
    bsz = t // seq
    bf = jnp.pad(b_f.astype(F32), (0, LANES - b_f.shape[0])).reshape(1, LANES)
    return pl.pallas_call(
        _fox_gate_kernel,
        grid=(bsz, seq // chunk),
        in_specs=[pl.BlockSpec((chunk, LANES), lambda b, i: (b * (seq // chunk) + i, 0)),
                  pl.BlockSpec((1, LANES), lambda b, i: (0, 0))],
        out_specs=pl.BlockSpec((chunk, LANES), lambda b, i: (b * (seq // chunk) + i, 0)),
        out_shape=jax.ShapeDtypeStruct((t, LANES), F32),
        scratch_shapes=[pltpu.VMEM((1, LANES), F32)],
        compiler_params=_params(("arbitrary", "arbitrary"), 16),
        name="fox_gate",
    )(fg, bf)


def _fox_attn_kernel(q_ref, k_ref, v_ref, fq_ref, fk_ref, o_ref, *, tq):
    p = pl.program_id(1)
    qi = pl.program_id(2)
    lane = lax.broadcasted_iota(jnp.int32, (1, LANES), 1)
    first = lane < HEAD_DIM
    q = q_ref[...]
    zero = jnp.zeros_like(q)
    q_h = (jnp.where(first, q, zero), jnp.where(first, zero, q))
    fq_all = fq_ref[...]
    fq_h = tuple(jnp.sum(jnp.where(lane == 2 * p + t, fq_all, 0.0), axis=1, keepdims=True) for t in range(2))
    causal = lax.broadcasted_iota(jnp.int32, (tq, tq), 1) <= lax.broadcasted_iota(jnp.int32, (tq, tq), 0)

    def step(j, carry, masked):
        off = pl.multiple_of(j * tq, tq)
        k = k_ref[pl.ds(off, tq), :]
        v = v_ref[pl.ds(off, tq), :]
        out = []
        for t in range(2):
            m, l, acc = carry[t]
            fk = fk_ref[t:t + 1, pl.ds(off, tq)]
            s = lax.dot_general(q_h[t], k, (((1,), (1,)), ((), ())), preferred_element_type=F32) - fk
            if masked:
                s = jnp.where(causal, s, NEG_INF)
            m_new = jnp.maximum(m, jnp.max(s, axis=1, keepdims=True) + fq_h[t])
            pexp = jnp.exp(s + (fq_h[t] - m_new))
            alpha = jnp.exp(m - m_new)
            l = alpha * l + jnp.sum(pexp, axis=1, keepdims=True)
            acc = alpha * acc + _dot(pexp.astype(BF16), v)
            out.append((m_new, l, acc))
        return tuple(out)

    init = tuple((jnp.full((tq, 1), NEG_INF, F32), jnp.zeros((tq, 1), F32), jnp.zeros((tq, LANES), F32))
                 for _ in range(2))
    carry = lax.fori_loop(0, qi, lambda j, c: step(j, c, False), init)
    (_, l_a, acc_a), (_, l_b, acc_b) = step(qi, carry, True)
    o_ref[...] = jnp.where(first, acc_a / l_a, acc_b / l_b).astype(o_ref.dtype)


def _fox_attn(qkv, f_q, f_k, *, seq, d_attn, tq=256):
    t = qkv.shape[0]
    bsz = t // seq
    n_pairs = d_attn // LANES
    nq = seq // tq
    kern = functools.partial(_fox_attn_kernel, tq=tq)
    return pl.pallas_call(
        kern,
        grid=(bsz, n_pairs, nq),
        in_specs=[pl.BlockSpec((tq, LANES), lambda b, p, i: (b * nq + i, p)),
                  pl.BlockSpec((seq, LANES), lambda b, p, i: (b, n_pairs + p)),
                  pl.BlockSpec((seq, LANES), lambda b, p, i: (b, 2 * n_pairs + p)),
                  pl.BlockSpec((tq, LANES), lambda b, p, i: (b * nq + i, 0)),
                  pl.BlockSpec((None, None, 2, seq), lambda b, p, i: (b, p, 0, 0))],
        out_specs=pl.BlockSpec((tq, LANES), lambda b, p, i: (b * nq + i, p)),
        out_shape=jax.ShapeDtypeStruct((t, d_attn), BF16),
        compiler_params=_params(("arbitrary", "arbitrary", "arbitrary"), 32),
        name="fox_attn",
    )(qkv, qkv, qkv, f_q, f_k)


def _conv_kernel(cur_ref, halo_ref, w_ref, cb_ref, lg_ref, lb_ref, o_ref, ybuf, cbuf, *, ts):
    qi = pl.program_id(1)
    c = cur_ref.shape[1]
    halo = halo_ref[...].astype(F32)
    ybuf[0:CONV_HALO, :] = jnp.where(qi > 0, halo, 0.0)
    ybuf[CONV_HALO:CONV_HALO + ts, :] = cur_ref[...].astype(F32)

    def lane_chunk(ci, _):
        lo = pl.multiple_of(ci * LANES, LANES)
        for r in range(ts // CONV_ROWS):
            acc = jnp.broadcast_to(cb_ref[:, pl.ds(lo, LANES)], (CONV_ROWS, LANES))
            for j in range(CONV_K):
                start = r * CONV_ROWS + CONV_HALO - (CONV_K - 1) + j
                acc = acc + w_ref[j:j + 1, pl.ds(lo, LANES)] * ybuf[start:start + CONV_ROWS, pl.ds(lo, LANES)]
            cbuf[r * CONV_ROWS:(r + 1) * CONV_ROWS, pl.ds(lo, LANES)] = acc
        return 0

    lax.fori_loop(0, c // LANES, lane_chunk, 0)
    y = cbuf[...]
    mu = jnp.mean(y, axis=-1, keepdims=True)
    yc = y - mu
    z = yc * lax.rsqrt(jnp.mean(yc * yc, axis=-1, keepdims=True) + EPS) * lg_ref[...] + lb_ref[...]
    o_ref[...] = _silu(z).astype(o_ref.dtype)


def _conv_branch(y, conv_w, conv_b, ln_g, ln_b, *, seq, ts=256):
    t, c = y.shape
    bsz = t // seq
    ns = seq // ts
    w = jnp.pad(conv_w.astype(F32), ((0, CONV_HALO - CONV_K), (0, 0)))
    kern = functools.partial(_conv_kernel, ts=ts)
    row = lambda a: a.astype(F32).reshape(1, c)
    return pl.pallas_call(
        kern,
        grid=(bsz, ns),
        in_specs=[pl.BlockSpec((ts, c), lambda b, i: (b * ns + i, 0)),
                  pl.BlockSpec((CONV_HALO, c),
                               lambda b, i: (jnp.maximum((b * ns + i) * (ts // CONV_HALO) - 1, 0), 0)),
                  pl.BlockSpec((CONV_HALO, c), lambda b, i: (0, 0)),
                  pl.BlockSpec((1, c), lambda b, i: (0, 0)),
                  pl.BlockSpec((1, c), lambda b, i: (0, 0)),
                  pl.BlockSpec((1, c), lambda b, i: (0, 0))],
        out_specs=pl.BlockSpec((ts, c), lambda b, i: (b * ns + i, 0)),
        out_shape=jax.ShapeDtypeStruct((t, c), BF16),
        scratch_shapes=[pltpu.VMEM((CONV_HALO + ts, c), F32), pltpu.VMEM((ts, c), F32)],
        compiler_params=_params(("arbitrary", "arbitrary"), 32),
        name="conv_branch",
    )(y, y, w, row(conv_b), row(ln_g), row(ln_b))


def _mix_out_kernel(z_ref, o_ref, sga_ref, sgb_ref, co_ref, ao_ref, wo_ref, x_ref, mod_ref, out_ref, acc_ref):
    j = pl.program_id(1)
    y_a = _dot(z_ref[...], co_ref[...])
    y_b = _dot(o_ref[...], ao_ref[...])
    merged = (sga_ref[...].astype(F32) * y_a + sgb_ref[...].astype(F32) * y_b).astype(BF16)
    contrib = _dot(merged, wo_ref[...])

    @pl.when(j == 0)
    def _():
        acc_ref[...] = contrib

    @pl.when(j > 0)
    def _():
        acc_ref[...] += contrib

    @pl.when(j == pl.num_programs(1) - 1)
    def _():
        out_ref[...] = x_ref[...] + mod_ref[2:3, :] * acc_ref[...]


def _mix_out(z, o, gates, conv_out, attn_out, w_out, x, mod, *, seq, tm=512, tn=512):
    t, d = x.shape
    c = z.shape[1]
    nj = d // tn
    tiles_per_batch = seq // tm
    return pl.pallas_call(
        _mix_out_kernel,
        grid=(t // tm, nj),
        in_specs=[pl.BlockSpec((tm, c), lambda i, j: (i, 0)),
                  pl.BlockSpec((tm, c), lambda i, j: (i, 0)),
                  pl.BlockSpec((tm, tn), lambda i, j: (i, j)),
                  pl.BlockSpec((tm, tn), lambda i, j: (i, nj + j)),
                  pl.BlockSpec((c, tn), lambda i, j: (0, j)),
                  pl.BlockSpec((c, tn), lambda i, j: (0, j)),
                  pl.BlockSpec((tn, d), lambda i, j: (j, 0)),
                  pl.BlockSpec((tm, d), lambda i, j: (i, 0)),
                  pl.BlockSpec((None, 6, d), lambda i, j: (i // tiles_per_batch, 0, 0))],
        out_specs=pl.BlockSpec((tm, d), lambda i, j: (i, 0)),
        out_shape=jax.ShapeDtypeStruct((t, d), F32),
        scratch_shapes=[pltpu.VMEM((tm, d), F32)],
        compiler_params=_params(("arbitrary", "arbitrary"), 48),
        name="mix_out",
    )(z, o, gates, gates, conv_out, attn_out, w_out, x, mod)


def _ffn_kernel(x_ref, mod_ref, g_ref, w1_ref, w3_ref, w2_ref, out_ref, h_ref, acc_ref):
    j = pl.program_id(1)

    @pl.when(j == 0)
    def _():
        h_ref[...] = _norm_mod(x_ref[...], g_ref[...], mod_ref[3:4, :], mod_ref[4:5, :]).astype(BF16)

    h = h_ref[...]
    hidden = (_silu(_dot(h, w1_ref[...])) * _dot(h, w3_ref[...])).astype(BF16)
    contrib = _dot(hidden, w2_ref[...])

    @pl.when(j == 0)
    def _():
        acc_ref[...] = contrib

    @pl.when(j > 0)
    def _():
        acc_ref[...] += contrib

    @pl.when(j == pl.num_programs(1) - 1)
    def _():
        out_ref[...] = x_ref[...] + mod_ref[5:6, :] * acc_ref[...]


def _ffn(x, mod, g, w1, w3, w2, *, seq, tm=512, tf=512):
    t, d = x.shape
    f = w1.shape[1]
    tiles_per_batch = seq // tm
    return pl.pallas_call(
        _ffn_kernel,
        grid=(t // tm, f // tf),
        in_specs=[pl.BlockSpec((tm, d), lambda i, j: (i, 0)),
                  pl.BlockSpec((None, 6, d), lambda i, j: (i // tiles_per_batch, 0, 0)),
                  pl.BlockSpec((1, d), lambda i, j: (0, 0)),
                  pl.BlockSpec((d, tf), lambda i, j: (0, j)),
                  pl.BlockSpec((d, tf), lambda i, j: (0, j)),
                  pl.BlockSpec((tf, d), lambda i, j: (j, 0))],
        out_specs=pl.BlockSpec((tm, d), lambda i, j: (i, 0)),
        out_shape=jax.ShapeDtypeStruct((t, d), F32),
        scratch_shapes=[pltpu.VMEM((tm, d), BF16), pltpu.VMEM((tm, d), F32)],
        compiler_params=_params(("arbitrary", "arbitrary"), 48),
        name="ffn",
    )(x, mod, g, w1, w3, w2)


def _router_kernel(x_ref, mod_ref, g_ref, rw_ref, rb_ref, h_ref, idx_ref, wt_ref):
    h = _norm_mod(x_ref[...], g_ref[...], mod_ref[3:4, :], mod_ref[4:5, :])
    h_ref[...] = h
    logits = jnp.dot(h, rw_ref[...], preferred_element_type=F32, precision=lax.Precision.HIGHEST) + rb_ref[...]
    lane = lax.broadcasted_iota(jnp.int32, logits.shape, 1).astype(F32)
    lg = jnp.where(lane < N_EXPERTS, logits, -jnp.inf)
    v1 = jnp.max(lg, axis=1, keepdims=True)
    i1 = jnp.min(jnp.where(lg == v1, lane, float(LANES)), axis=1, keepdims=True)
    lg2 = jnp.where(lane == i1, -jnp.inf, lg)
    v2 = jnp.max(lg2, axis=1, keepdims=True)
    i2 = jnp.min(jnp.where(lg2 == v2, lane, float(LANES)), axis=1, keepdims=True)
    e2 = jnp.exp(v2 - v1)
    w1 = 1.0 / (1.0 + e2)
    w2 = e2 / (1.0 + e2)
    idx_ref[...] = jnp.where(lane == 0.0, i1, jnp.where(lane == 1.0, i2, 0.0)).astype(jnp.int32)
    wt_ref[...] = jnp.where(lane == 0.0, w1, jnp.where(lane == 1.0, w2, 0.0))


def _router(x, mod, g, router_w, router_b, *, seq, tm=512):
    t, d = x.shape
    tiles_per_batch = seq // tm
    rw = jnp.pad(router_w.astype(F32), ((0, 0), (0, LANES - N_EXPERTS)))
    rb = jnp.pad(router_b.astype(F32), (0, LANES - N_EXPERTS)).reshape(1, LANES)
    return pl.pallas_call(
        _router_kernel,
        grid=(t // tm,),
        in_specs=[pl.BlockSpec((tm, d), lambda i: (i, 0)),
                  pl.BlockSpec((None, 6, d), lambda i: (i // tiles_per_batch, 0, 0)),
                  pl.BlockSpec((1, d), lambda i: (0, 0)),
                  pl.BlockSpec((d, LANES), lambda i: (0, 0)),
                  pl.BlockSpec((1, LANES), lambda i: (0, 0))],
        out_specs=[pl.BlockSpec((tm, d), lambda i: (i, 0)),
                   pl.BlockSpec((tm, LANES), lambda i: (i, 0)),
                   pl.BlockSpec((tm, LANES), lambda i: (i, 0))],
        out_shape=[jax.ShapeDtypeStruct((t, d), F32),
                   jax.ShapeDtypeStruct((t, LANES), jnp.int32),
                   jax.ShapeDtypeStruct((t, LANES), F32)],
        compiler_params=_params(("arbitrary",), 40),
        name="router",
    )(x, mod, g, rw, rb)


def _dispatch_tables(idx, wts, *, tm):
    t = idx.shape[0]
    n_rows = TOP_K * t + N_EXPERTS * tm
    n_tiles = n_rows // tm
    flat_e = idx.reshape(-1)
    onehot = (flat_e[:, None] == jnp.arange(N_EXPERTS, dtype=jnp.int32)[None, :]).astype(jnp.int32)
    rank = jnp.sum((jnp.cumsum(onehot, axis=0) - onehot) * onehot, axis=1)
    counts = jnp.sum(onehot, axis=0)
    padded = ((counts + tm - 1) // tm) * tm
    ends = jnp.cumsum(padded)
    starts = ends - padded
    pos = starts[flat_e] + rank
    token = jnp.arange(TOP_K * t, dtype=jnp.int32) // TOP_K
    slot = jnp.arange(TOP_K * t, dtype=jnp.int32) % TOP_K
    row_token = jnp.zeros((n_rows,), jnp.int32).at[pos].set(token)
    row_dst = jnp.full((n_rows,), -1, jnp.int32).at[pos].set(slot * t + token)
    row_gate = jnp.zeros((n_rows,), F32).at[pos].set(wts.reshape(-1)).reshape(n_rows, 1)
    n_active = (ends[-1] // tm).astype(jnp.int32)
    tile_start = jnp.minimum(jnp.arange(n_tiles, dtype=jnp.int32), n_active - 1) * tm
    tile_expert = jnp.minimum(jnp.searchsorted(ends, tile_start, side="right"), N_EXPERTS - 1).astype(jnp.int32)
    return tile_expert, n_active.reshape(1), row_token, row_dst, row_gate


def _moe_kernel(te_ref, na_ref, tok_ref, dst_ref, h_hbm, gate_ref, w1_ref, w3_ref, w2_ref, ys_hbm,
                xbuf, xbf, acc_ref, sem_in, sem_out, *, tm):
    i = pl.program_id(0)
    j = pl.program_id(1)
    active = i < na_ref[0]
    base = i * tm

    def row_in(r, tok):
        return pltpu.make_async_copy(h_hbm.at[pl.ds(tok, 1)], xbuf.at[pl.ds(r, 1)], sem_in)

    def row_out(r, dst):
        return pltpu.make_async_copy(xbuf.at[pl.ds(r, 1)], ys_hbm.at[pl.ds(dst, 1)], sem_out)

    @pl.when(active & (j == 0))
    def _():
        def issue(r, _):
            row_in(r, tok_ref[base + r]).start()
            return 0

        def wait(r, _):
            row_in(r, 0).wait()
            return 0

        lax.fori_loop(0, tm, issue, 0)
        lax.fori_loop(0, tm, wait, 0)
        xbf[...] = xbuf[...].astype(BF16)

    @pl.when(active)
    def _():
        x = xbf[...]
        hidden = (_silu(_dot(x, w1_ref[...])) * _dot(x, w3_ref[...])).astype(BF16)
        contrib = _dot(hidden, w2_ref[...])

        @pl.when(j == 0)
        def _():
            acc_ref[...] = contrib

        @pl.when(j > 0)
        def _():
            acc_ref[...] += contrib

    @pl.when(active & (j == pl.num_programs(1) - 1))
    def _():
        xbuf[...] = acc_ref[...] * gate_ref[...]

        def issue(r, _):
            dst = dst_ref[base + r]

            @pl.when(dst >= 0)
            def _():
                row_out(r, dst).start()
            return 0

        def wait(r, _):
            dst = dst_ref[base + r]

            @pl.when(dst >= 0)
            def _():
                row_out(r, dst).wait()
            return 0

        lax.fori_loop(0, tm, issue, 0)
        lax.fori_loop(0, tm, wait, 0)


def _moe_experts(h, tables, w1, w3, w2, *, tm=512, tf=256):
    t, d = h.shape
    f = w1.shape[2]
    nf = f // tf
    tile_expert, n_active, row_token, row_dst, row_gate = tables
    n_tiles = tile_expert.shape[0]
    kern = functools.partial(_moe_kernel, tm=tm)

    def f_idx(i, j, na):
        return jnp.where(i < na[0], j, nf - 1)

    grid_spec = pltpu.PrefetchScalarGridSpec(
        num_scalar_prefetch=4,
        grid=(n_tiles, nf),
        in_specs=[pl.BlockSpec(memory_space=pl.ANY),
                  pl.BlockSpec((tm, 1), lambda i, j, te, na, tok, dst: (i, 0)),
                  pl.BlockSpec((None, d, tf), lambda i, j, te, na, tok, dst: (te[i], 0, f_idx(i, j, na))),
                  pl.BlockSpec((None, d, tf), lambda i, j, te, na, tok, dst: (te[i], 0, f_idx(i, j, na))),
                  pl.BlockSpec((None, tf, d), lambda i, j, te, na, tok, dst: (te[i], f_idx(i, j, na), 0))],
        out_specs=pl.BlockSpec(memory_space=pl.ANY),
        scratch_shapes=[pltpu.VMEM((tm, d), F32), pltpu.VMEM((tm, d), BF16), pltpu.VMEM((tm, d), F32),
                        pltpu.SemaphoreType.DMA(()), pltpu.SemaphoreType.DMA(())],
    )
    return pl.pallas_call(
        kern,
        grid_spec=grid_spec,
        out_shape=jax.ShapeDtypeStruct((TOP_K * t, d), F32),
        compiler_params=_params(("arbitrary", "arbitrary"), 48),
        name="moe_experts",
    )(tile_expert, n_active, row_token, row_dst, h, row_gate, w1, w3, w2)


def _combine_kernel(x_ref, y0_ref, y1_ref, mod_ref, out_ref):
    out_ref[...] = x_ref[...] + mod_ref[5:6, :] * (y0_ref[...] + y1_ref[...])


def _combine(x, ys, mod, *, seq, tm=512):
    t, d = x.shape
    nt = t // tm
    tiles_per_batch = seq // tm
    return pl.pallas_call(
        _combine_kernel,
        grid=(nt,),
        in_specs=[pl.BlockSpec((tm, d), lambda i: (i, 0)),
                  pl.BlockSpec((tm, d), lambda i: (i, 0)),
                  pl.BlockSpec((tm, d), lambda i: (nt + i, 0)),
                  pl.BlockSpec((None, 6, d), lambda i: (i // tiles_per_batch, 0, 0))],
        out_specs=pl.BlockSpec((tm, d), lambda i: (i, 0)),
        out_shape=jax.ShapeDtypeStruct((t, d), F32),
        compiler_params=_params(("arbitrary",), 48),
        name="moe_combine",
    )(x, ys, ys, mod)


def _final_norm_kernel(x_ref, g_ref, o_ref):
    x = x_ref[...]
    o_ref[...] = x * lax.rsqrt(jnp.mean(x * x, axis=-1, keepdims=True) + EPS) * g_ref[...]


def _final_norm(x, g, tm=512):
    t, d = x.shape
    return pl.pallas_call(
        _final_norm_kernel,
        grid=(t // tm,),
        in_specs=[pl.BlockSpec((tm, d), lambda i: (i, 0)), pl.BlockSpec((1, d), lambda i: (0, 0))],
        out_specs=pl.BlockSpec((tm, d), lambda i: (i, 0)),
        out_shape=jax.ShapeDtypeStruct((t, d), F32),
        compiler_params=_params(("arbitrary",), 32),
        name="final_norm",
    )(x, g)


def kernel(x, c, ada_w, ada_b, norm_mix_g, norm_ffn_g, w_in, b_f, conv_w, conv_b, conv_ln_g, conv_ln_b,
           conv_out, attn_out, w_out, ffn_w1, ffn_w3, ffn_w2, router_w, router_b, moe_w1, moe_w3, moe_w2,
           final_norm_g):
    bsz, seq, d = x.shape
    depth = ada_w.shape[0]
    d_conv = conv_w.shape[2]
    d_attn = attn_out.shape[1]
    t = bsz * seq
    moe_tm = 512

    mods = _adaln(c, ada_w, ada_b).reshape(depth, bsz, 6, d)
    xf = x.reshape(t, d)
    for l in range(depth):
        mod = mods[l]
        w_all, w_fg = _arrange_w_in(w_in[l], d_attn=d_attn, d_conv=d_conv)
        qkv, glu, gates, fg = _mix_in(xf, mod, norm_mix_g[l].reshape(1, d), w_all, w_fg,
                                      seq=seq, d_attn=d_attn, d_conv=d_conv)
        z = _conv_branch(glu, conv_w[l], conv_b[l], conv_ln_g[l], conv_ln_b[l], seq=seq)
        f_q = _fox_gate(fg, b_f[l], seq=seq)
        f_k = f_q[:, :N_HEADS].reshape(bsz, seq, N_HEADS // 2, 2).transpose(0, 2, 3, 1)
        o = _fox_attn(qkv, f_q, f_k, seq=seq, d_attn=d_attn)
        xf = _mix_out(z, o, gates, conv_out[l].astype(BF16), attn_out[l].astype(BF16), w_out[l].astype(BF16),
                      xf, mod, seq=seq)
        g_ffn = norm_ffn_g[l].reshape(1, d)
        i = l // 2
        if l % 2 == 0:
            xf = _ffn(xf, mod, g_ffn, ffn_w1[i].astype(BF16), ffn_w3[i].astype(BF16), ffn_w2[i].astype(BF16),
                      seq=seq)
        else:
            h, idx, wts = _router(xf, mod, g_ffn, router_w[i], router_b[i], seq=seq)
            tables = _dispatch_tables(idx[:, :TOP_K], wts[:, :TOP_K], tm=moe_tm)
            ys = _moe_experts(h, tables, moe_w1[i].astype(BF16), moe_w3[i].astype(BF16), moe_w2[i].astype(BF16),
                              tm=moe_tm)
            xf = _combine(xf, ys, mod, seq=seq)
    return _final_norm(xf, final_norm_g.reshape(1, d)).reshape(bsz, seq, d)
```

```python
import functools

import jax
import jax.numpy as jnp
from jax import lax
from jax.experimental import pallas as pl
from jax.experimental.pallas import tpu as pltpu

EPS = 1e-6
NEG_INF = -1e30
LOG2E = 1.4426950408889634
N_HEADS = 16
HEAD_DIM = 64
CONV_K = 31
N_EXPERTS = 8
TOP_K = 2

LANES = 128
CONV_HALO = 32
CONV_ROWS = 64
MIB = 1024 * 1024

F32 = jnp.float32
BF16 = jnp.bfloat16


def _params(semantics, vmem_mib):
    return pltpu.CompilerParams(dimension_semantics=semantics, vmem_limit_bytes=vmem_mib * MIB)


def _dot(a, b):
    return jnp.dot(a, b, preferred_element_type=F32)


def _silu(x):
    return x * jax.nn.sigmoid(x)


def _norm_mod(x, g, shift, scale):
    y = x * lax.rsqrt(jnp.mean(x * x, axis=-1, keepdims=True) + EPS) * g
    return y * (1.0 + scale) + shift


def _adaln_kernel(ct_ref, w_ref, b_ref, o_ref):
    w = w_ref[...]
    for b in range(o_ref.shape[0]):
        col = _silu(ct_ref[:, b:b + 1])
        o_ref[b:b + 1, :] = jnp.sum(w * col, axis=0, keepdims=True) + b_ref[...]


def _adaln(c, ada_w, ada_b, tn=1024):
    depth, d, n = ada_w.shape
    bsz = c.shape[0]
    return pl.pallas_call(
        _adaln_kernel,
        grid=(depth, n // tn),
        in_specs=[pl.BlockSpec((d, bsz), lambda l, j: (0, 0)),
                  pl.BlockSpec((None, d, tn), lambda l, j: (l, 0, j)),
                  pl.BlockSpec((None, 1, tn), lambda l, j: (l, 0, j))],
        out_specs=pl.BlockSpec((None, bsz, tn), lambda l, j: (l, 0, j)),
        out_shape=jax.ShapeDtypeStruct((depth, bsz, n), F32),
        compiler_params=_params(("arbitrary", "arbitrary"), 40),
        name="adaln",
    )(c.T, ada_w, ada_b.reshape(depth, 1, n))


def _mix_in_kernel(x_ref, mod_ref, g_ref, wqkv_ref, wa_ref, wg_ref, wgate_ref, wfg_ref,
                   qkv_ref, glu_ref, gate_ref, fg_ref, h_ref, *, n_q, n_qkv, n_glu):
    j = pl.program_id(1)

    @pl.when(j == 0)
    def _():
        h = _norm_mod(x_ref[...], g_ref[...], mod_ref[0:1, :], mod_ref[1:2, :]).astype(BF16)
        h_ref[...] = h
        fg_ref[...] = _dot(h, wfg_ref[...])

    @pl.when(j < n_q)
    def _():
        qkv_ref[...] = (_dot(h_ref[...], wqkv_ref[...]) * (HEAD_DIM ** -0.5 * LOG2E)).astype(BF16)

    @pl.when((j >= n_q) & (j < n_qkv))
    def _():
        qkv_ref[...] = _dot(h_ref[...], wqkv_ref[...]).astype(BF16)

    @pl.when((j >= n_qkv) & (j < n_qkv + n_glu))
    def _():
        h = h_ref[...]
        glu_ref[...] = (_dot(h, wa_ref[...]) * jax.nn.sigmoid(_dot(h, wg_ref[...]))).astype(BF16)

    @pl.when(j >= n_qkv + n_glu)
    def _():
        gate_ref[...] = jax.nn.sigmoid(_dot(h_ref[...], wgate_ref[...])).astype(BF16)


def _mix_in(x, mod, g, w_parts, *, seq, tm=1024, tn=512):
    t, d = x.shape
    w_qkv, w_a, w_g, w_gate, w_fg = w_parts
    d_attn, d_conv = w_qkv.shape[1] // 3, w_a.shape[1]
    n_q, n_qkv, n_glu, n_gate = d_attn // tn, 3 * d_attn // tn, 2 * d_conv // tn, 2 * d // tn
    tiles_per_batch = seq // tm
    glu_idx = lambda j: jnp.clip(j - n_qkv, 0, n_glu - 1)
    gate_idx = lambda j: jnp.clip(j - n_qkv - n_glu, 0, n_gate - 1)
    kern = functools.partial(_mix_in_kernel, n_q=n_q, n_qkv=n_qkv, n_glu=n_glu)
    return pl.pallas_call(
        kern,
        grid=(t // tm, n_qkv + n_glu + n_gate),
        in_specs=[pl.BlockSpec((tm, d), lambda i, j: (i, 0)),
                  pl.BlockSpec((None, 6, d), lambda i, j: (i // tiles_per_batch, 0, 0)),
                  pl.BlockSpec((1, d), lambda i, j: (0, 0)),
                  pl.BlockSpec((d, tn), lambda i, j: (0, jnp.minimum(j, n_qkv - 1))),
                  pl.BlockSpec((d, tn // 2), lambda i, j: (0, glu_idx(j))),
                  pl.BlockSpec((d, tn // 2), lambda i, j: (0, glu_idx(j))),
                  pl.BlockSpec((d, tn), lambda i, j: (0, gate_idx(j))),
                  pl.BlockSpec((d, LANES), lambda i, j: (0, 0))],
        out_specs=[pl.BlockSpec((tm, tn), lambda i, j: (i, jnp.minimum(j, n_qkv - 1))),
                   pl.BlockSpec((tm, tn // 2), lambda i, j: (i, glu_idx(j))),
                   pl.BlockSpec((tm, tn), lambda i, j: (i, gate_idx(j))),
                   pl.BlockSpec((tm, LANES), lambda i, j: (i, 0))],
        out_shape=[jax.ShapeDtypeStruct((t, 3 * d_attn), BF16),
                   jax.ShapeDtypeStruct((t, d_conv), BF16),
                   jax.ShapeDtypeStruct((t, 2 * d), BF16),
                   jax.ShapeDtypeStruct((t, LANES), F32)],
        scratch_shapes=[pltpu.VMEM((tm, d), BF16)],
        compiler_params=_params(("arbitrary", "arbitrary"), 48),
        name="mix_in",
    )(x, mod, g, w_qkv, w_a, w_g, w_gate, w_fg)


def _split_w_in(w, *, d_attn, d_conv):
    o = 3 * d_attn
    w_fg = jnp.pad(w[:, o:o + N_HEADS].astype(BF16), ((0, 0), (0, LANES - N_HEADS)))
    u = o + N_HEADS
    g = u + 2 * d_conv
    return (w[:, :o].astype(BF16), w[:, u:u + d_conv].astype(BF16), w[:, u + d_conv:g].astype(BF16),
            w[:, g:].astype(BF16), w_fg)


def _fox_gate_kernel(fg_ref, bf_ref, o_ref, carry_ref):
    @pl.when(pl.program_id(1) == 0)
    def _():
        carry_ref[...] = jnp.zeros_like(carry_ref)

    x = fg_ref[...] + bf_ref[...]
    logf = jnp.minimum(x, 0.0) - jnp.log(1.0 + jnp.exp(-jnp.abs(x)))
    n = x.shape[0]
    tri = (lax.broadcasted_iota(jnp.int32, (n, n), 1) <= lax.broadcasted_iota(jnp.int32, (n, n), 0)).astype(BF16)
    hi = logf.astype(BF16)
    r1 = logf - hi.astype(F32)
    mid = r1.astype(BF16)
    lo = (r1 - mid.astype(F32)).astype(BF16)
    cs = _dot(tri, hi) + _dot(tri, mid) + _dot(tri, lo) + carry_ref[...]
    o_ref[...] = cs
    carry_ref[...] = cs[n - 1:n, :]


def _fox_gate(fg, b_f, *, seq, chunk=256):
    t = fg.shape[0]
    bsz = t // seq
    bf = jnp.pad(b_f.astype(F32), (0, LANES - b_f.shape[0])).reshape(1, LANES)
    return pl.pallas_call(
        _fox_gate_kernel,
        grid=(bsz, seq // chunk),
        in_specs=[pl.BlockSpec((chunk, LANES), lambda b, i: (b * (seq // chunk) + i, 0)),
                  pl.BlockSpec((1, LANES), lambda b, i: (0, 0))],
        out_specs=pl.BlockSpec((chunk, LANES), lambda b, i: (b * (seq // chunk) + i, 0)),
        out_shape=jax.ShapeDtypeStruct((t, LANES), F32),
        scratch_shapes=[pltpu.VMEM((1, LANES), F32)],
        compiler_params=_params(("arbitrary", "arbitrary"), 16),
        name="fox_gate",
    )(fg, bf)


def _split3(x):
    hi = x.astype(BF16)
    r = x - hi.astype(F32)
    mid = r.astype(BF16)
    return hi, mid, (r - mid.astype(F32)).astype(BF16)


def _fox_attn_kernel(q_ref, k_ref, v_ref, f_ref, o_ref, kx_ref, vx_ref, *, tq):
    p = pl.program_id(1)
    qi = pl.program_id(2)
    lane = lax.broadcasted_iota(jnp.int32, (1, LANES), 1)
    first = lane < HEAD_DIM
    data = (first, jnp.logical_not(first))
    spare = (HEAD_DIM, 0)

    @pl.when(qi == 0)
    def _():
        k = k_ref[...].astype(F32)
        v = v_ref[...].astype(F32)
        f_all = f_ref[...]
        for t in range(2):
            f_key = jnp.sum(jnp.where(lane == 2 * p + t, f_all, 0.0), axis=1, keepdims=True) * (-LOG2E)
            hi, mid, lo = (piece.astype(F32) for piece in _split3(f_key))
            bias = jnp.where(lane == spare[t], hi,
                             jnp.where(lane == spare[t] + 1, mid, jnp.where(lane == spare[t] + 2, lo, 0.0)))
            kx_ref[t] = jnp.where(data[t], k, bias).astype(BF16)
            vx_ref[t] = jnp.where(data[t], v, jnp.where(lane == spare[t], 1.0, 0.0)).astype(BF16)

    q = q_ref[...].astype(F32)
    f_rows = f_ref[pl.ds(pl.multiple_of(qi * tq, tq), tq), :]
    q_h, fq_h = [], []
    for t in range(2):
        ones = (lane >= spare[t]) & (lane < spare[t] + 3)
        q_h.append(jnp.where(data[t], q, jnp.where(ones, 1.0, 0.0)).astype(BF16))
        fq_h.append(jnp.sum(jnp.where(lane == 2 * p + t, f_rows, 0.0), axis=1, keepdims=True) * LOG2E)
    causal = lax.broadcasted_iota(jnp.int32, (tq, tq), 1) <= lax.broadcasted_iota(jnp.int32, (tq, tq), 0)

    def step(j, carry, masked):
        off = pl.multiple_of(j * tq, tq)
        out = []
        for t in range(2):
            m, acc = carry[t]
            s = lax.dot_general(q_h[t], kx_ref[t, pl.ds(off, tq), :], (((1,), (1,)), ((), ())),
                                preferred_element_type=F32)
            if masked:
                s = jnp.where(causal, s, NEG_INF)
            m_new = jnp.maximum(m, jnp.max(s, axis=1, keepdims=True) + fq_h[t])
            pexp = jnp.exp2(s + (fq_h[t] - m_new))
            acc = jnp.exp2(m - m_new) * acc + _dot(pexp.astype(BF16), vx_ref[t, pl.ds(off, tq), :])
            out.append((m_new, acc))
        return tuple(out)

    init = tuple((jnp.full((tq, 1), NEG_INF, F32), jnp.zeros((tq, LANES), F32)) for _ in range(2))
    carry = lax.fori_loop(0, qi, lambda j, c: step(j, c, False), init)
    (_, acc_a), (_, acc_b) = step(qi, carry, True)
    out_a = acc_a / acc_a[:, spare[0]:spare[0] + 1]
    out_b = acc_b / acc_b[:, spare[1]:spare[1] + 1]
    o_ref[...] = jnp.where(first, out_a, out_b).astype(o_ref.dtype)


def _fox_attn(qkv, f_cum, *, seq, d_attn, tq=512):
    t = qkv.shape[0]
    bsz = t // seq
    n_pairs = d_attn // LANES
    nq = seq // tq
    kern = functools.partial(_fox_attn_kernel, tq=tq)
    return pl.pallas_call(
        kern,
        grid=(bsz, n_pairs, nq),
        in_specs=[pl.BlockSpec((tq, LANES), lambda b, p, i: (b * nq + i, p)),
                  pl.BlockSpec((seq, LANES), lambda b, p, i: (b, n_pairs + p)),
                  pl.BlockSpec((seq, LANES), lambda b, p, i: (b, 2 * n_pairs + p)),
                  pl.BlockSpec((seq, LANES), lambda b, p, i: (b, 0))],
        out_specs=pl.BlockSpec((tq, LANES), lambda b, p, i: (b * nq + i, p)),
        out_shape=jax.ShapeDtypeStruct((t, d_attn), BF16),
        scratch_shapes=[pltpu.VMEM((2, seq, LANES), BF16), pltpu.VMEM((2, seq, LANES), BF16)],
        compiler_params=_params(("arbitrary", "arbitrary", "arbitrary"), 40),
        name="fox_attn",
    )(qkv, qkv, qkv, f_cum)


def _conv_kernel(cur_ref, halo_ref, w_ref, cb_ref, lg_ref, lb_ref, o_ref, ybuf, cbuf, *, ts):
    qi = pl.program_id(1)
    c = cur_ref.shape[1]
    halo = halo_ref[...].astype(F32)
    ybuf[0:CONV_HALO, :] = jnp.where(qi > 0, halo, 0.0)
    ybuf[CONV_HALO:CONV_HALO + ts, :] = cur_ref[...].astype(F32)

    def lane_chunk(ci, _):
        lo = pl.multiple_of(ci * LANES, LANES)
        for r in range(ts // CONV_ROWS):
            acc = jnp.broadcast_to(cb_ref[:, pl.ds(lo, LANES)], (CONV_ROWS, LANES))
            for j in range(CONV_K):
                start = r * CONV_ROWS + CONV_HALO - (CONV_K - 1) + j
                acc = acc + w_ref[j:j + 1, pl.ds(lo, LANES)] * ybuf[start:start + CONV_ROWS, pl.ds(lo, LANES)]
            cbuf[r * CONV_ROWS:(r + 1) * CONV_ROWS, pl.ds(lo, LANES)] = acc
        return 0

    lax.fori_loop(0, c // LANES, lane_chunk, 0)
    y = cbuf[...]
    mu = jnp.mean(y, axis=-1, keepdims=True)
    yc = y - mu
    z = yc * lax.rsqrt(jnp.mean(yc * yc, axis=-1, keepdims=True) + EPS) * lg_ref[...] + lb_ref[...]
    o_ref[...] = _silu(z).astype(o_ref.dtype)


def _conv_branch(y, conv_w, conv_b, ln_g, ln_b, *, seq, ts=256):
    t, c = y.shape
    bsz = t // seq
    ns = seq // ts
    w = jnp.pad(conv_w.astype(F32), ((0, CONV_HALO - CONV_K), (0, 0)))
    kern = functools.partial(_conv_kernel, ts=ts)
    row = lambda a: a.astype(F32).reshape(1, c)
    return pl.pallas_call(
        kern,
        grid=(bsz, ns),
        in_specs=[pl.BlockSpec((ts, c), lambda b, i: (b * ns + i, 0)),
                  pl.BlockSpec((CONV_HALO, c),
                               lambda b, i: (jnp.maximum((b * ns + i) * (ts // CONV_HALO) - 1, 0), 0)),
                  pl.BlockSpec((CONV_HALO, c), lambda b, i: (0, 0)),
                  pl.BlockSpec((1, c), lambda b, i: (0, 0)),
                  pl.BlockSpec((1, c), lambda b, i: (0, 0)),
                  pl.BlockSpec((1, c), lambda b, i: (0, 0))],
        out_specs=pl.BlockSpec((ts, c), lambda b, i: (b * ns + i, 0)),
        out_shape=jax.ShapeDtypeStruct((t, c), BF16),
        scratch_shapes=[pltpu.VMEM((CONV_HALO + ts, c), F32), pltpu.VMEM((ts, c), F32)],
        compiler_params=_params(("arbitrary", "arbitrary"), 32),
        name="conv_branch",
    )(y, y, w, row(conv_b), row(ln_g), row(ln_b))


def _mix_out_kernel(z_ref, o_ref, sga_ref, sgb_ref, co_ref, ao_ref, wo_ref, x_ref, mod_ref, out_ref, acc_ref):
    j = pl.program_id(1)
    y_a = _dot(z_ref[...], co_ref[...])
    y_b = _dot(o_ref[...], ao_ref[...])
    merged = (sga_ref[...].astype(F32) * y_a + sgb_ref[...].astype(F32) * y_b).astype(BF16)
    contrib = _dot(merged, wo_ref[...])

    @pl.when(j == 0)
    def _():
        acc_ref[...] = contrib

    @pl.when(j > 0)
    def _():
        acc_ref[...] += contrib

    @pl.when(j == pl.num_programs(1) - 1)
    def _():
        out_ref[...] = x_ref[...] + mod_ref[2:3, :] * acc_ref[...]


def _mix_out(z, o, gates, conv_out, attn_out, w_out, x, mod, *, seq, tm=512, tn=512):
    t, d = x.shape
    c = z.shape[1]
    nj = d // tn
    tiles_per_batch = seq // tm
    return pl.pallas_call(
        _mix_out_kernel,
        grid=(t // tm, nj),
        in_specs=[pl.BlockSpec((tm, c), lambda i, j: (i, 0)),
                  pl.BlockSpec((tm, c), lambda i, j: (i, 0)),
                  pl.BlockSpec((tm, tn), lambda i, j: (i, j)),
                  pl.BlockSpec((tm, tn), lambda i, j: (i, nj + j)),
                  pl.BlockSpec((c, tn), lambda i, j: (0, j)),
                  pl.BlockSpec((c, tn), lambda i, j: (0, j)),
                  pl.BlockSpec((tn, d), lambda i, j: (j, 0)),
                  pl.BlockSpec((tm, d), lambda i, j: (i, 0)),
                  pl.BlockSpec((None, 6, d), lambda i, j: (i // tiles_per_batch, 0, 0))],
        out_specs=pl.BlockSpec((tm, d), lambda i, j: (i, 0)),
        out_shape=jax.ShapeDtypeStruct((t, d), F32),
        scratch_shapes=[pltpu.VMEM((tm, d), F32)],
        compiler_params=_params(("arbitrary", "arbitrary"), 48),
        name="mix_out",
    )(z, o, gates, gates, conv_out, attn_out, w_out, x, mod)


def _ffn_kernel(x_ref, mod_ref, g_ref, w1_ref, w3_ref, w2_ref, out_ref, h_ref, acc_ref):
    j = pl.program_id(1)

    @pl.when(j == 0)
    def _():
        h_ref[...] = _norm_mod(x_ref[...], g_ref[...], mod_ref[3:4, :], mod_ref[4:5, :]).astype(BF16)

    h = h_ref[...]
    hidden = (_silu(_dot(h, w1_ref[...])) * _dot(h, w3_ref[...])).astype(BF16)
    contrib = _dot(hidden, w2_ref[...])

    @pl.when(j == 0)
    def _():
        acc_ref[...] = contrib

    @pl.when(j > 0)
    def _():
        acc_ref[...] += contrib

    @pl.when(j == pl.num_programs(1) - 1)
    def _():
        out_ref[...] = x_ref[...] + mod_ref[5:6, :] * acc_ref[...]


def _ffn(x, mod, g, w1, w3, w2, *, seq, tm=512, tf=512):
    t, d = x.shape
    f = w1.shape[1]
    tiles_per_batch = seq // tm
    return pl.pallas_call(
        _ffn_kernel,
        grid=(t // tm, f // tf),
        in_specs=[pl.BlockSpec((tm, d), lambda i, j: (i, 0)),
                  pl.BlockSpec((None, 6, d), lambda i, j: (i // tiles_per_batch, 0, 0)),
                  pl.BlockSpec((1, d), lambda i, j: (0, 0)),
                  pl.BlockSpec((d, tf), lambda i, j: (0, j)),
                  pl.BlockSpec((d, tf), lambda i, j: (0, j)),
                  pl.BlockSpec((tf, d), lambda i, j: (j, 0))],
        out_specs=pl.BlockSpec((tm, d), lambda i, j: (i, 0)),
        out_shape=jax.ShapeDtypeStruct((t, d), F32),
        scratch_shapes=[pltpu.VMEM((tm, d), BF16), pltpu.VMEM((tm, d), F32)],
        compiler_params=_params(("arbitrary", "arbitrary"), 48),
        name="ffn",
    )(x, mod, g, w1, w3, w2)


def _router_kernel(x_ref, mod_ref, g_ref, rw_ref, rb_ref, h_ref, idx_ref, wt_ref):
    h = _norm_mod(x_ref[...], g_ref[...], mod_ref[3:4, :], mod_ref[4:5, :])
    h_ref[...] = h
    logits = jnp.dot(h, rw_ref[...], preferred_element_type=F32, precision=lax.Precision.HIGHEST) + rb_ref[...]
    lane = lax.broadcasted_iota(jnp.int32, logits.shape, 1).astype(F32)
    lg = jnp.where(lane < N_EXPERTS, logits, -jnp.inf)
    v1 = jnp.max(lg, axis=1, keepdims=True)
    i1 = jnp.min(jnp.where(lg == v1, lane, float(LANES)), axis=1, keepdims=True)
    lg2 = jnp.where(lane == i1, -jnp.inf, lg)
    v2 = jnp.max(lg2, axis=1, keepdims=True)
    i2 = jnp.min(jnp.where(lg2 == v2, lane, float(LANES)), axis=1, keepdims=True)
    e2 = jnp.exp(v2 - v1)
    w1 = 1.0 / (1.0 + e2)
    w2 = e2 / (1.0 + e2)
    idx_ref[...] = jnp.where(lane == 0.0, i1, jnp.where(lane == 1.0, i2, 0.0)).astype(jnp.int32)
    wt_ref[...] = jnp.where(lane == 0.0, w1, jnp.where(lane == 1.0, w2, 0.0))


def _router(x, mod, g, router_w, router_b, *, seq, tm=512):
    t, d = x.shape
    tiles_per_batch = seq // tm
    rw = jnp.pad(router_w.astype(F32), ((0, 0), (0, LANES - N_EXPERTS)))
    rb = jnp.pad(router_b.astype(F32), (0, LANES - N_EXPERTS)).reshape(1, LANES)
    return pl.pallas_call(
        _router_kernel,
        grid=(t // tm,),
        in_specs=[pl.BlockSpec((tm, d), lambda i: (i, 0)),
                  pl.BlockSpec((None, 6, d), lambda i: (i // tiles_per_batch, 0, 0)),
                  pl.BlockSpec((1, d), lambda i: (0, 0)),
                  pl.BlockSpec((d, LANES), lambda i: (0, 0)),
                  pl.BlockSpec((1, LANES), lambda i: (0, 0))],
        out_specs=[pl.BlockSpec((tm, d), lambda i: (i, 0)),
                   pl.BlockSpec((tm, LANES), lambda i: (i, 0)),
                   pl.BlockSpec((tm, LANES), lambda i: (i, 0))],
        out_shape=[jax.ShapeDtypeStruct((t, d), F32),
                   jax.ShapeDtypeStruct((t, LANES), jnp.int32),
                   jax.ShapeDtypeStruct((t, LANES), F32)],
        compiler_params=_params(("arbitrary",), 40),
        name="router",
    )(x, mod, g, rw, rb)


def _dispatch_tables(idx, wts, *, tm):
    t = idx.shape[0]
    n_rows = TOP_K * t + N_EXPERTS * tm
    n_tiles = n_rows // tm
    flat_e = idx.reshape(-1)
    onehot = (flat_e[:, None] == jnp.arange(N_EXPERTS, dtype=jnp.int32)[None, :]).astype(jnp.int32)
    rank = jnp.sum((jnp.cumsum(onehot, axis=0) - onehot) * onehot, axis=1)
    counts = jnp.sum(onehot, axis=0)
    padded = ((counts + tm - 1) // tm) * tm
    ends = jnp.cumsum(padded)
    starts = ends - padded
    pos = starts[flat_e] + rank
    row_src = jnp.full((n_rows,), -1, jnp.int32).at[pos].set(jnp.arange(TOP_K * t, dtype=jnp.int32))
    valid = row_src >= 0
    src = jnp.maximum(row_src, 0)
    row_token = src // TOP_K
    row_dst = jnp.where(valid, (src % TOP_K) * t + row_token, -1)
    row_gate = jnp.where(valid, wts.reshape(-1)[src], 0.0).reshape(n_rows, 1)
    n_active = (ends[-1] // tm).astype(jnp.int32)
    tile_start = jnp.minimum(jnp.arange(n_tiles, dtype=jnp.int32), n_active - 1) * tm
    tile_expert = jnp.minimum(jnp.sum((tile_start[:, None] >= ends[None, :]).astype(jnp.int32), axis=1),
                              N_EXPERTS - 1)
    tile_rows = jnp.sum(valid.reshape(n_tiles, tm).astype(jnp.int32), axis=1)
    return tile_expert, n_active.reshape(1), row_token, row_dst, tile_rows, row_gate


def _moe_kernel(te_ref, na_ref, tok_ref, dst_ref, rows_ref, h_hbm, gate_ref, w1_ref, w3_ref, w2_ref, ys_hbm,
                xbuf, xbf, acc_ref, sem_in, sem_out, *, tm):
    i = pl.program_id(0)
    j = pl.program_id(1)
    active = i < na_ref[0]
    base = i * tm

    def row_in(r, tok):
        return pltpu.make_async_copy(h_hbm.at[pl.ds(tok, 1)], xbuf.at[pl.ds(r, 1)], sem_in)

    def row_out(r, dst):
        return pltpu.make_async_copy(xbuf.at[pl.ds(r, 1)], ys_hbm.at[pl.ds(dst, 1)], sem_out)

    @pl.when(active & (j == 0))
    def _():
        def issue(r, _):
            row_in(r, tok_ref[base + r]).start()
            return 0

        lax.fori_loop(0, tm, issue, 0, unroll=8)
        pltpu.make_async_copy(h_hbm.at[pl.ds(0, tm)], xbuf, sem_in).wait()
        xbf[...] = xbuf[...].astype(BF16)

    @pl.when(active)
    def _():
        x = xbf[...]
        hidden = (_silu(_dot(x, w1_ref[...])) * _dot(x, w3_ref[...])).astype(BF16)
        contrib = _dot(hidden, w2_ref[...])

        @pl.when(j == 0)
        def _():
            acc_ref[...] = contrib

        @pl.when(j > 0)
        def _():
            acc_ref[...] += contrib

    @pl.when(active & (j == pl.num_programs(1) - 1))
    def _():
        xbuf[...] = acc_ref[...] * gate_ref[...]
        n_rows = rows_ref[i]

        def issue(r, _):
            row_out(r, dst_ref[base + r]).start()
            return 0

        def wait(r, _):
            row_out(r, 0).wait()
            return 0

        lax.fori_loop(0, n_rows, issue, 0)
        lax.fori_loop(0, n_rows, wait, 0)


def _moe_experts(h, tables, w1, w3, w2, *, tm=512, tf=256):
    t, d = h.shape
    f = w1.shape[2]
    nf = f // tf
    tile_expert, n_active, row_token, row_dst, tile_rows, row_gate = tables
    n_tiles = tile_expert.shape[0]
    kern = functools.partial(_moe_kernel, tm=tm)

    def f_idx(i, j, na):
        return jnp.where(i < na[0], j, nf - 1)

    grid_spec = pltpu.PrefetchScalarGridSpec(
        num_scalar_prefetch=5,
        grid=(n_tiles, nf),
        in_specs=[pl.BlockSpec(memory_space=pl.ANY),
                  pl.BlockSpec((tm, 1), lambda i, j, te, na, *_: (i, 0)),
                  pl.BlockSpec((None, d, tf), lambda i, j, te, na, *_: (te[i], 0, f_idx(i, j, na))),
                  pl.BlockSpec((None, d, tf), lambda i, j, te, na, *_: (te[i], 0, f_idx(i, j, na))),
                  pl.BlockSpec((None, tf, d), lambda i, j, te, na, *_: (te[i], f_idx(i, j, na), 0))],
        out_specs=pl.BlockSpec(memory_space=pl.ANY),
        scratch_shapes=[pltpu.VMEM((tm, d), F32), pltpu.VMEM((tm, d), BF16), pltpu.VMEM((tm, d), F32),
                        pltpu.SemaphoreType.DMA(()), pltpu.SemaphoreType.DMA(())],
    )
    return pl.pallas_call(
        kern,
        grid_spec=grid_spec,
        out_shape=jax.ShapeDtypeStruct((TOP_K * t, d), F32),
        compiler_params=_params(("arbitrary", "arbitrary"), 48),
        name="moe_experts",
    )(tile_expert, n_active, row_token, row_dst, tile_rows, h, row_gate, w1, w3, w2)


def _combine_kernel(x_ref, y0_ref, y1_ref, mod_ref, out_ref):
    out_ref[...] = x_ref[...] + mod_ref[5:6, :] * (y0_ref[...] + y1_ref[...])


def _combine(x, ys, mod, *, seq, tm=512):
    t, d = x.shape
    nt = t // tm
    tiles_per_batch = seq // tm
    return pl.pallas_call(
        _combine_kernel,
        grid=(nt,),
        in_specs=[pl.BlockSpec((tm, d), lambda i: (i, 0)),
                  pl.BlockSpec((tm, d), lambda i: (i, 0)),
                  pl.BlockSpec((tm, d), lambda i: (nt + i, 0)),
                  pl.BlockSpec((None, 6, d), lambda i: (i // tiles_per_batch, 0, 0))],
        out_specs=pl.BlockSpec((tm, d), lambda i: (i, 0)),
        out_shape=jax.ShapeDtypeStruct((t, d), F32),
        compiler_params=_params(("arbitrary",), 48),
        name="moe_combine",
    )(x, ys, ys, mod)


def _final_norm_kernel(x_ref, g_ref, o_ref):
    x = x_ref[...]
    o_ref[...] = x * lax.rsqrt(jnp.mean(x * x, axis=-1, keepdims=True) + EPS) * g_ref[...]


def _final_norm(x, g, tm=512):
    t, d = x.shape
    return pl.pallas_call(
        _final_norm_kernel,
        grid=(t // tm,),
        in_specs=[pl.BlockSpec((tm, d), lambda i: (i, 0)), pl.BlockSpec((1, d), lambda i: (0, 0))],
        out_specs=pl.BlockSpec((tm, d), lambda i: (i, 0)),
        out_shape=jax.ShapeDtypeStruct((t, d), F32),
        compiler_params=_params(("arbitrary",), 32),
        name="final_norm",
    )(x, g)


def kernel(x, c, ada_w, ada_b, norm_mix_g, norm_ffn_g, w_in, b_f, conv_w, conv_b, conv_ln_g, conv_ln_b,
           conv_out, attn_out, w_out, ffn_w1, ffn_w3, ffn_w2, router_w, router_b, moe_w1, moe_w3, moe_w2,
           final_norm_g):
    bsz, seq, d = x.shape
    depth = ada_w.shape[0]
    d_conv = conv_w.shape[2]
    d_attn = attn_out.shape[1]
    t = bsz * seq
    moe_tm = 512

    mods = _adaln(c, ada_w, ada_b).reshape(depth, bsz, 6, d)
    xf = x.reshape(t, d)
    for l in range(depth):
        mod = mods[l]
        w_parts = _split_w_in(w_in[l], d_attn=d_attn, d_conv=d_conv)
        qkv, glu, gates, fg = _mix_in(xf, mod, norm_mix_g[l].reshape(1, d), w_parts, seq=seq)
        z = _conv_branch(glu, conv_w[l], conv_b[l], conv_ln_g[l], conv_ln_b[l], seq=seq)
        o = _fox_attn(qkv, _fox_gate(fg, b_f[l], seq=seq), seq=seq, d_attn=d_attn)
        xf = _mix_out(z, o, gates, conv_out[l].astype(BF16), attn_out[l].astype(BF16), w_out[l].astype(BF16),
                      xf, mod, seq=seq)
        g_ffn = norm_ffn_g[l].reshape(1, d)
        i = l // 2
        if l % 2 == 0:
            xf = _ffn(xf, mod, g_ffn, ffn_w1[i].astype(BF16), ffn_w3[i].astype(BF16), ffn_w2[i].astype(BF16),
                      seq=seq)
        else:
            h, idx, wts = _router(xf, mod, g_ffn, router_w[i], router_b[i], seq=seq)
            tables = _dispatch_tables(idx[:, :TOP_K], wts[:, :TOP_K], tm=moe_tm)
            ys = _moe_experts(h, tables, moe_w1[i].astype(BF16), moe_w3[i].astype(BF16), moe_w2[i].astype(BF16),
                              tm=moe_tm)
            xf = _combine(xf, ys, mod, seq=seq)
    return _final_norm(xf, final_norm_g.reshape(1, d)).reshape(bsz, seq, d)
```

```python
import functools

import jax
import jax.numpy as jnp
from jax import lax
from jax.experimental import pallas as pl
from jax.experimental.pallas import tpu as pltpu

EPS = 1e-6
NEG_INF = -1e30
LOG2E = 1.4426950408889634
N_HEADS = 16
HEAD_DIM = 64
CONV_K = 31
N_EXPERTS = 8
TOP_K = 2

LANES = 128
CONV_HALO = 32
CONV_ROWS = 64
MIB = 1024 * 1024

F32 = jnp.float32
BF16 = jnp.bfloat16


def _params(semantics, vmem_mib):
    return pltpu.CompilerParams(dimension_semantics=semantics, vmem_limit_bytes=vmem_mib * MIB)


def _dot(a, b):
    return jnp.dot(a, b, preferred_element_type=F32)


def _silu(x):
    return x * jax.nn.sigmoid(x)


def _norm_mod(x, g, shift, scale):
    y = x * lax.rsqrt(jnp.mean(x * x, axis=-1, keepdims=True) + EPS) * g
    return y * (1.0 + scale) + shift


def _adaln_kernel(ct_ref, w_ref, b_ref, o_ref):
    w = w_ref[...]
    for b in range(o_ref.shape[0]):
        col = _silu(ct_ref[:, b:b + 1])
        o_ref[b:b + 1, :] = jnp.sum(w * col, axis=0, keepdims=True) + b_ref[...]


def _adaln(c, ada_w, ada_b, tn=1024):
    depth, d, n = ada_w.shape
    bsz = c.shape[0]
    return pl.pallas_call(
        _adaln_kernel,
        grid=(depth, n // tn),
        in_specs=[pl.BlockSpec((d, bsz), lambda l, j: (0, 0)),
                  pl.BlockSpec((None, d, tn), lambda l, j: (l, 0, j)),
                  pl.BlockSpec((None, 1, tn), lambda l, j: (l, 0, j))],
        out_specs=pl.BlockSpec((None, bsz, tn), lambda l, j: (l, 0, j)),
        out_shape=jax.ShapeDtypeStruct((depth, bsz, n), F32),
        compiler_params=_params(("arbitrary", "arbitrary"), 40),
        name="adaln",
    )(c.T, ada_w, ada_b.reshape(depth, 1, n))


def _mix_in_kernel(x_ref, mod_ref, g_ref, wqkv_ref, wa_ref, wg_ref, wgate_ref, wfg_ref,
                   qkv_ref, glu_ref, gate_ref, fg_ref, h_ref, *, n_q, n_qkv, n_glu):
    j = pl.program_id(1)

    @pl.when(j == 0)
    def _():
        h = _norm_mod(x_ref[...], g_ref[...], mod_ref[0:1, :], mod_ref[1:2, :]).astype(BF16)
        h_ref[...] = h
        fg_ref[...] = _dot(h, wfg_ref[...])

    @pl.when(j < n_q)
    def _():
        qkv_ref[...] = (_dot(h_ref[...], wqkv_ref[...]) * (HEAD_DIM ** -0.5 * LOG2E)).astype(BF16)

    @pl.when((j >= n_q) & (j < n_qkv))
    def _():
        qkv_ref[...] = _dot(h_ref[...], wqkv_ref[...]).astype(BF16)

    @pl.when((j >= n_qkv) & (j < n_qkv + n_glu))
    def _():
        h = h_ref[...]
        glu_ref[...] = (_dot(h, wa_ref[...]) * jax.nn.sigmoid(_dot(h, wg_ref[...]))).astype(BF16)

    @pl.when(j >= n_qkv + n_glu)
    def _():
        gate_ref[...] = jax.nn.sigmoid(_dot(h_ref[...], wgate_ref[...])).astype(BF16)


def _mix_in(x, mod, g, w_parts, *, seq, tm=1024, tn=512):
    t, d = x.shape
    w_qkv, w_a, w_g, w_gate, w_fg = w_parts
    d_attn, d_conv = w_qkv.shape[1] // 3, w_a.shape[1]
    n_q, n_qkv, n_glu, n_gate = d_attn // tn, 3 * d_attn // tn, 2 * d_conv // tn, 2 * d // tn
    tiles_per_batch = seq // tm
    glu_idx = lambda j: jnp.clip(j - n_qkv, 0, n_glu - 1)
    gate_idx = lambda j: jnp.clip(j - n_qkv - n_glu, 0, n_gate - 1)
    kern = functools.partial(_mix_in_kernel, n_q=n_q, n_qkv=n_qkv, n_glu=n_glu)
    return pl.pallas_call(
        kern,
        grid=(t // tm, n_qkv + n_glu + n_gate),
        in_specs=[pl.BlockSpec((tm, d), lambda i, j: (i, 0)),
                  pl.BlockSpec((None, 6, d), lambda i, j: (i // tiles_per_batch, 0, 0)),
                  pl.BlockSpec((1, d), lambda i, j: (0, 0)),
                  pl.BlockSpec((d, tn), lambda i, j: (0, jnp.minimum(j, n_qkv - 1))),
                  pl.BlockSpec((d, tn // 2), lambda i, j: (0, glu_idx(j))),
                  pl.BlockSpec((d, tn // 2), lambda i, j: (0, glu_idx(j))),
                  pl.BlockSpec((d, tn), lambda i, j: (0, gate_idx(j))),
                  pl.BlockSpec((d, LANES), lambda i, j: (0, 0))],
        out_specs=[pl.BlockSpec((tm, tn), lambda i, j: (i, jnp.minimum(j, n_qkv - 1))),
                   pl.BlockSpec((tm, tn // 2), lambda i, j: (i, glu_idx(j))),
                   pl.BlockSpec((tm, tn), lambda i, j: (i, gate_idx(j))),
                   pl.BlockSpec((tm, LANES), lambda i, j: (i, 0))],
        out_shape=[jax.ShapeDtypeStruct((t, 3 * d_attn), BF16),
                   jax.ShapeDtypeStruct((t, d_conv), BF16),
                   jax.ShapeDtypeStruct((t, 2 * d), BF16),
                   jax.ShapeDtypeStruct((t, LANES), F32)],
        scratch_shapes=[pltpu.VMEM((tm, d), BF16)],
        compiler_params=_params(("arbitrary", "arbitrary"), 48),
        name="mix_in",
    )(x, mod, g, w_qkv, w_a, w_g, w_gate, w_fg)


def _split_w_in(w, *, d_attn, d_conv):
    o = 3 * d_attn
    w_fg = jnp.pad(w[:, o:o + N_HEADS].astype(BF16), ((0, 0), (0, LANES - N_HEADS)))
    u = o + N_HEADS
    g = u + 2 * d_conv
    return (w[:, :o].astype(BF16), w[:, u:u + d_conv].astype(BF16), w[:, u + d_conv:g].astype(BF16),
            w[:, g:].astype(BF16), w_fg)


def _fox_gate_kernel(fg_ref, bf_ref, o_ref, carry_ref):
    @pl.when(pl.program_id(1) == 0)
    def _():
        carry_ref[...] = jnp.zeros_like(carry_ref)

    x = fg_ref[...] + bf_ref[...]
    logf = jnp.minimum(x, 0.0) - jnp.log(1.0 + jnp.exp(-jnp.abs(x)))
    n = x.shape[0]
    tri = (lax.broadcasted_iota(jnp.int32, (n, n), 1) <= lax.broadcasted_iota(jnp.int32, (n, n), 0)).astype(BF16)
    hi = logf.astype(BF16)
    r1 = logf - hi.astype(F32)
    mid = r1.astype(BF16)
    lo = (r1 - mid.astype(F32)).astype(BF16)
    cs = _dot(tri, hi) + _dot(tri, mid) + _dot(tri, lo) + carry_ref[...]
    o_ref[...] = cs
    carry_ref[...] = cs[n - 1:n, :]


def _fox_gate(fg, b_f, *, seq, chunk=256):
    t = fg.shape[0]
    bsz = t // seq
    bf = jnp.pad(b_f.astype(F32), (0, LANES - b_f.shape[0])).reshape(1, LANES)
    return pl.pallas_call(
        _fox_gate_kernel,
        grid=(bsz, seq // chunk),
        in_specs=[pl.BlockSpec((chunk, LANES), lambda b, i: (b * (seq // chunk) + i, 0)),
                  pl.BlockSpec((1, LANES), lambda b, i: (0, 0))],
        out_specs=pl.BlockSpec((chunk, LANES), lambda b, i: (b * (seq // chunk) + i, 0)),
        out_shape=jax.ShapeDtypeStruct((t, LANES), F32),
        scratch_shapes=[pltpu.VMEM((1, LANES), F32)],
        compiler_params=_params(("arbitrary", "arbitrary"), 16),
        name="fox_gate",
    )(fg, bf)


def _split3(x):
    hi = x.astype(BF16)
    r = x - hi.astype(F32)
    mid = r.astype(BF16)
    return hi, mid, (r - mid.astype(F32)).astype(BF16)


def _fox_attn_kernel(q_ref, k_ref, v_ref, f_ref, o_ref, kx_ref, vx_ref, *, tq):
    p = pl.program_id(1)
    qi = pl.program_id(2)
    lane = lax.broadcasted_iota(jnp.int32, (1, LANES), 1)
    first = lane < HEAD_DIM
    data = (first, jnp.logical_not(first))
    spare = (HEAD_DIM, 0)

    @pl.when(qi == 0)
    def _():
        k = k_ref[...].astype(F32)
        v = v_ref[...].astype(F32)
        f_all = f_ref[...]
        for t in range(2):
            f_key = jnp.sum(jnp.where(lane == 2 * p + t, f_all, 0.0), axis=1, keepdims=True) * (-LOG2E)
            hi, mid, lo = (piece.astype(F32) for piece in _split3(f_key))
            bias = jnp.where(lane == spare[t], hi,
                             jnp.where(lane == spare[t] + 1, mid, jnp.where(lane == spare[t] + 2, lo, 0.0)))
            kx_ref[t] = jnp.where(data[t], k, bias).astype(BF16)
            vx_ref[t] = jnp.where(data[t], v, jnp.where(lane == spare[t], 1.0, 0.0)).astype(BF16)

    q = q_ref[...].astype(F32)
    f_rows = f_ref[pl.ds(pl.multiple_of(qi * tq, tq), tq), :]
    q_h, fq_h = [], []
    for t in range(2):
        ones = (lane >= spare[t]) & (lane < spare[t] + 3)
        q_h.append(jnp.where(data[t], q, jnp.where(ones, 1.0, 0.0)).astype(BF16))
        fq_h.append(jnp.sum(jnp.where(lane == 2 * p + t, f_rows, 0.0), axis=1, keepdims=True) * LOG2E)
    causal = lax.broadcasted_iota(jnp.int32, (tq, tq), 1) <= lax.broadcasted_iota(jnp.int32, (tq, tq), 0)

    def step(j, carry, masked):
        off = pl.multiple_of(j * tq, tq)
        out = []
        for t in range(2):
            m, acc = carry[t]
            s = lax.dot_general(q_h[t], kx_ref[t, pl.ds(off, tq), :], (((1,), (1,)), ((), ())),
                                preferred_element_type=F32)
            if masked:
                s = jnp.where(causal, s, NEG_INF)
            m_new = jnp.maximum(m, jnp.max(s, axis=1, keepdims=True) + fq_h[t])
            pexp = jnp.exp2(s + (fq_h[t] - m_new))
            acc = jnp.exp2(m - m_new) * acc + _dot(pexp.astype(BF16), vx_ref[t, pl.ds(off, tq), :])
            out.append((m_new, acc))
        return tuple(out)

    init = tuple((jnp.full((tq, 1), NEG_INF, F32), jnp.zeros((tq, LANES), F32)) for _ in range(2))
    carry = lax.fori_loop(0, qi, lambda j, c: step(j, c, False), init)
    (_, acc_a), (_, acc_b) = step(qi, carry, True)
    out_a = acc_a / acc_a[:, spare[0]:spare[0] + 1]
    out_b = acc_b / acc_b[:, spare[1]:spare[1] + 1]
    o_ref[...] = jnp.where(first, out_a, out_b).astype(o_ref.dtype)


def _fox_attn(qkv, f_cum, *, seq, d_attn, tq=512):
    t = qkv.shape[0]
    bsz = t // seq
    n_pairs = d_attn // LANES
    nq = seq // tq
    kern = functools.partial(_fox_attn_kernel, tq=tq)
    return pl.pallas_call(
        kern,
        grid=(bsz, n_pairs, nq),
        in_specs=[pl.BlockSpec((tq, LANES), lambda b, p, i: (b * nq + i, p)),
                  pl.BlockSpec((seq, LANES), lambda b, p, i: (b, n_pairs + p)),
                  pl.BlockSpec((seq, LANES), lambda b, p, i: (b, 2 * n_pairs + p)),
                  pl.BlockSpec((seq, LANES), lambda b, p, i: (b, 0))],
        out_specs=pl.BlockSpec((tq, LANES), lambda b, p, i: (b * nq + i, p)),
        out_shape=jax.ShapeDtypeStruct((t, d_attn), BF16),
        scratch_shapes=[pltpu.VMEM((2, seq, LANES), BF16), pltpu.VMEM((2, seq, LANES), BF16)],
        compiler_params=_params(("arbitrary", "arbitrary", "arbitrary"), 40),
        name="fox_attn",
    )(qkv, qkv, qkv, f_cum)


def _conv_kernel(cur_ref, halo_ref, w_ref, cb_ref, lg_ref, lb_ref, o_ref, ybuf, cbuf, *, ts):
    qi = pl.program_id(1)
    c = cur_ref.shape[1]
    halo = halo_ref[...].astype(F32)
    ybuf[0:CONV_HALO, :] = jnp.where(qi > 0, halo, 0.0)
    ybuf[CONV_HALO:CONV_HALO + ts, :] = cur_ref[...].astype(F32)

    def lane_chunk(ci, _):
        lo = pl.multiple_of(ci * LANES, LANES)
        for r in range(ts // CONV_ROWS):
            acc = jnp.broadcast_to(cb_ref[:, pl.ds(lo, LANES)], (CONV_ROWS, LANES))
            for j in range(CONV_K):
                start = r * CONV_ROWS + CONV_HALO - (CONV_K - 1) + j
                acc = acc + w_ref[j:j + 1, pl.ds(lo, LANES)] * ybuf[start:start + CONV_ROWS, pl.ds(lo, LANES)]
            cbuf[r * CONV_ROWS:(r + 1) * CONV_ROWS, pl.ds(lo, LANES)] = acc
        return 0

    lax.fori_loop(0, c // LANES, lane_chunk, 0)
    y = cbuf[...]
    mu = jnp.mean(y, axis=-1, keepdims=True)
    yc = y - mu
    z = yc * lax.rsqrt(jnp.mean(yc * yc, axis=-1, keepdims=True) + EPS) * lg_ref[...] + lb_ref[...]
    o_ref[...] = _silu(z).astype(o_ref.dtype)


def _conv_branch(y, conv_w, conv_b, ln_g, ln_b, *, seq, ts=256):
    t, c = y.shape
    bsz = t // seq
    ns = seq // ts
    w = jnp.pad(conv_w.astype(F32), ((0, CONV_HALO - CONV_K), (0, 0)))
    kern = functools.partial(_conv_kernel, ts=ts)
    row = lambda a: a.astype(F32).reshape(1, c)
    return pl.pallas_call(
        kern,
        grid=(bsz, ns),
        in_specs=[pl.BlockSpec((ts, c), lambda b, i: (b * ns + i, 0)),
                  pl.BlockSpec((CONV_HALO, c),
                               lambda b, i: (jnp.maximum((b * ns + i) * (ts // CONV_HALO) - 1, 0), 0)),
                  pl.BlockSpec((CONV_HALO, c), lambda b, i: (0, 0)),
                  pl.BlockSpec((1, c), lambda b, i: (0, 0)),
                  pl.BlockSpec((1, c), lambda b, i: (0, 0)),
                  pl.BlockSpec((1, c), lambda b, i: (0, 0))],
        out_specs=pl.BlockSpec((ts, c), lambda b, i: (b * ns + i, 0)),
        out_shape=jax.ShapeDtypeStruct((t, c), BF16),
        scratch_shapes=[pltpu.VMEM((CONV_HALO + ts, c), F32), pltpu.VMEM((ts, c), F32)],
        compiler_params=_params(("arbitrary", "arbitrary"), 32),
        name="conv_branch",
    )(y, y, w, row(conv_b), row(ln_g), row(ln_b))


def _mix_out_kernel(z_ref, o_ref, sga_ref, sgb_ref, co_ref, ao_ref, wo_ref, x_ref, mod_ref, out_ref, acc_ref):
    j = pl.program_id(1)
    y_a = _dot(z_ref[...], co_ref[...])
    y_b = _dot(o_ref[...], ao_ref[...])
    merged = (sga_ref[...].astype(F32) * y_a + sgb_ref[...].astype(F32) * y_b).astype(BF16)
    contrib = _dot(merged, wo_ref[...])

    @pl.when(j == 0)
    def _():
        acc_ref[...] = contrib

    @pl.when(j > 0)
    def _():
        acc_ref[...] += contrib

    @pl.when(j == pl.num_programs(1) - 1)
    def _():
        out_ref[...] = x_ref[...] + mod_ref[2:3, :] * acc_ref[...]


def _mix_out(z, o, gates, conv_out, attn_out, w_out, x, mod, *, seq, tm=512, tn=1024):
    t, d = x.shape
    c = z.shape[1]
    nj = d // tn
    tiles_per_batch = seq // tm
    return pl.pallas_call(
        _mix_out_kernel,
        grid=(t // tm, nj),
        in_specs=[pl.BlockSpec((tm, c), lambda i, j: (i, 0)),
                  pl.BlockSpec((tm, c), lambda i, j: (i, 0)),
                  pl.BlockSpec((tm, tn), lambda i, j: (i, j)),
                  pl.BlockSpec((tm, tn), lambda i, j: (i, nj + j)),
                  pl.BlockSpec((c, tn), lambda i, j: (0, j)),
                  pl.BlockSpec((c, tn), lambda i, j: (0, j)),
                  pl.BlockSpec((tn, d), lambda i, j: (j, 0)),
                  pl.BlockSpec((tm, d), lambda i, j: (i, 0)),
                  pl.BlockSpec((None, 6, d), lambda i, j: (i // tiles_per_batch, 0, 0))],
        out_specs=pl.BlockSpec((tm, d), lambda i, j: (i, 0)),
        out_shape=jax.ShapeDtypeStruct((t, d), F32),
        scratch_shapes=[pltpu.VMEM((tm, d), F32)],
        compiler_params=_params(("arbitrary", "arbitrary"), 48),
        name="mix_out",
    )(z, o, gates, gates, conv_out, attn_out, w_out, x, mod)


def _ffn_kernel(x_ref, mod_ref, g_ref, w1_ref, w3_ref, w2_ref, out_ref, h_ref, acc_ref):
    j = pl.program_id(1)

    @pl.when(j == 0)
    def _():
        h_ref[...] = _norm_mod(x_ref[...], g_ref[...], mod_ref[3:4, :], mod_ref[4:5, :]).astype(BF16)

    h = h_ref[...]
    hidden = (_silu(_dot(h, w1_ref[...])) * _dot(h, w3_ref[...])).astype(BF16)
    contrib = _dot(hidden, w2_ref[...])

    @pl.when(j == 0)
    def _():
        acc_ref[...] = contrib

    @pl.when(j > 0)
    def _():
        acc_ref[...] += contrib

    @pl.when(j == pl.num_programs(1) - 1)
    def _():
        out_ref[...] = x_ref[...] + mod_ref[5:6, :] * acc_ref[...]


def _ffn(x, mod, g, w1, w3, w2, *, seq, tm=512, tf=512):
    t, d = x.shape
    f = w1.shape[1]
    tiles_per_batch = seq // tm
    return pl.pallas_call(
        _ffn_kernel,
        grid=(t // tm, f // tf),
        in_specs=[pl.BlockSpec((tm, d), lambda i, j: (i, 0)),
                  pl.BlockSpec((None, 6, d), lambda i, j: (i // tiles_per_batch, 0, 0)),
                  pl.BlockSpec((1, d), lambda i, j: (0, 0)),
                  pl.BlockSpec((d, tf), lambda i, j: (0, j)),
                  pl.BlockSpec((d, tf), lambda i, j: (0, j)),
                  pl.BlockSpec((tf, d), lambda i, j: (j, 0))],
        out_specs=pl.BlockSpec((tm, d), lambda i, j: (i, 0)),
        out_shape=jax.ShapeDtypeStruct((t, d), F32),
        scratch_shapes=[pltpu.VMEM((tm, d), BF16), pltpu.VMEM((tm, d), F32)],
        compiler_params=_params(("arbitrary", "arbitrary"), 48),
        name="ffn",
    )(x, mod, g, w1, w3, w2)


def _router_kernel(x_ref, mod_ref, g_ref, rw_ref, rb_ref, h_ref, idx_ref, wt_ref):
    h = _norm_mod(x_ref[...], g_ref[...], mod_ref[3:4, :], mod_ref[4:5, :])
    h_ref[...] = h
    logits = jnp.dot(h, rw_ref[...], preferred_element_type=F32, precision=lax.Precision.HIGHEST) + rb_ref[...]
    lane = lax.broadcasted_iota(jnp.int32, logits.shape, 1).astype(F32)
    lg = jnp.where(lane < N_EXPERTS, logits, -jnp.inf)
    v1 = jnp.max(lg, axis=1, keepdims=True)
    i1 = jnp.min(jnp.where(lg == v1, lane, float(LANES)), axis=1, keepdims=True)
    lg2 = jnp.where(lane == i1, -jnp.inf, lg)
    v2 = jnp.max(lg2, axis=1, keepdims=True)
    i2 = jnp.min(jnp.where(lg2 == v2, lane, float(LANES)), axis=1, keepdims=True)
    e2 = jnp.exp(v2 - v1)
    w1 = 1.0 / (1.0 + e2)
    w2 = e2 / (1.0 + e2)
    idx_ref[...] = jnp.where(lane == 0.0, i1, jnp.where(lane == 1.0, i2, 0.0)).astype(jnp.int32)
    wt_ref[...] = jnp.where(lane == 0.0, w1, jnp.where(lane == 1.0, w2, 0.0))


def _router(x, mod, g, router_w, router_b, *, seq, tm=512):
    t, d = x.shape
    tiles_per_batch = seq // tm
    rw = jnp.pad(router_w.astype(F32), ((0, 0), (0, LANES - N_EXPERTS)))
    rb = jnp.pad(router_b.astype(F32), (0, LANES - N_EXPERTS)).reshape(1, LANES)
    return pl.pallas_call(
        _router_kernel,
        grid=(t // tm,),
        in_specs=[pl.BlockSpec((tm, d), lambda i: (i, 0)),
                  pl.BlockSpec((None, 6, d), lambda i: (i // tiles_per_batch, 0, 0)),
                  pl.BlockSpec((1, d), lambda i: (0, 0)),
                  pl.BlockSpec((d, LANES), lambda i: (0, 0)),
                  pl.BlockSpec((1, LANES), lambda i: (0, 0))],
        out_specs=[pl.BlockSpec((tm, d), lambda i: (i, 0)),
                   pl.BlockSpec((tm, LANES), lambda i: (i, 0)),
                   pl.BlockSpec((tm, LANES), lambda i: (i, 0))],
        out_shape=[jax.ShapeDtypeStruct((t, d), F32),
                   jax.ShapeDtypeStruct((t, LANES), jnp.int32),
                   jax.ShapeDtypeStruct((t, LANES), F32)],
        compiler_params=_params(("arbitrary",), 40),
        name="router",
    )(x, mod, g, rw, rb)


def _dispatch_tables(idx, wts, *, tm):
    t = idx.shape[0]
    n_rows = TOP_K * t + N_EXPERTS * tm
    n_tiles = n_rows // tm
    flat_e = idx.reshape(-1)
    onehot = (flat_e[:, None] == jnp.arange(N_EXPERTS, dtype=jnp.int32)[None, :]).astype(jnp.int32)
    rank = jnp.sum((jnp.cumsum(onehot, axis=0) - onehot) * onehot, axis=1)
    counts = jnp.sum(onehot, axis=0)
    padded = ((counts + tm - 1) // tm) * tm
    ends = jnp.cumsum(padded)
    starts = ends - padded
    pos = starts[flat_e] + rank
    row_src = jnp.full((n_rows,), -1, jnp.int32).at[pos].set(jnp.arange(TOP_K * t, dtype=jnp.int32))
    valid = row_src >= 0
    src = jnp.maximum(row_src, 0)
    row_token = src // TOP_K
    row_dst = jnp.where(valid, (src % TOP_K) * t + row_token, -1)
    row_gate = jnp.where(valid, wts.reshape(-1)[src], 0.0).reshape(n_rows, 1)
    n_active = (ends[-1] // tm).astype(jnp.int32)
    tile_start = jnp.minimum(jnp.arange(n_tiles, dtype=jnp.int32), n_active - 1) * tm
    tile_expert = jnp.minimum(jnp.sum((tile_start[:, None] >= ends[None, :]).astype(jnp.int32), axis=1),
                              N_EXPERTS - 1)
    tile_rows = jnp.sum(valid.reshape(n_tiles, tm).astype(jnp.int32), axis=1)
    return tile_expert, n_active.reshape(1), row_token, row_dst, tile_rows, row_gate


def _moe_kernel(te_ref, na_ref, tok_ref, dst_ref, rows_ref, h_hbm, gate_ref, w1_ref, w3_ref, w2_ref, ys_hbm,
                xbuf, xbf, acc_ref, sem_in, sem_out, *, tm, sub):
    i = pl.program_id(0)
    j = pl.program_id(1)
    last_j = pl.num_programs(1) - 1
    n_active = na_ref[0]
    active = i < n_active

    def gather_start(tile):
        base = tile * tm

        def issue(r, _):
            pltpu.make_async_copy(h_hbm.at[pl.ds(tok_ref[base + r], 1)], xbuf.at[pl.ds(r, 1)], sem_in).start()
            return 0

        lax.fori_loop(0, tm, issue, 0, unroll=8)

    def wait_scattered(n):
        n8 = lax.shift_right_logical(n, 3)

        @pl.when(n8 > 0)
        def _():
            rows = pl.ds(0, pl.multiple_of(n8 * 8, 8))
            pltpu.make_async_copy(acc_ref.at[rows], ys_hbm.at[rows], sem_out).wait()

        def wait_row(r, _):
            pltpu.make_async_copy(acc_ref.at[pl.ds(0, 1)], ys_hbm.at[pl.ds(0, 1)], sem_out).wait()
            return 0

        lax.fori_loop(n8 * 8, n, wait_row, 0)

    @pl.when(active & (j == 0))
    def _():
        @pl.when(i == 0)
        def _():
            gather_start(0)

        pltpu.make_async_copy(h_hbm.at[pl.ds(0, tm)], xbuf, sem_in).wait()
        xbf[...] = xbuf[...].astype(BF16)

        @pl.when(i + 1 < n_active)
        def _():
            gather_start(i + 1)

        @pl.when(i > 0)
        def _():
            wait_scattered(rows_ref[i - 1])

    @pl.when(active)
    def _():
        w1 = w1_ref[...].astype(BF16)
        w3 = w3_ref[...].astype(BF16)
        w2 = w2_ref[...].astype(BF16)
        n_valid = rows_ref[i]
        for s in range(tm // sub):
            rows = slice(s * sub, (s + 1) * sub)

            def compute(rows=rows):
                x = xbf[rows, :]
                hidden = (_silu(_dot(x, w1)) * _dot(x, w3)).astype(BF16)
                contrib = _dot(hidden, w2)

                @pl.when(j == 0)
                def _():
                    acc_ref[rows, :] = contrib

                @pl.when((j > 0) & (j < last_j))
                def _():
                    acc_ref[rows, :] += contrib

                @pl.when(j == last_j)
                def _():
                    acc_ref[rows, :] = (acc_ref[rows, :] + contrib) * gate_ref[rows, :]

            if s == 0:
                compute()
            else:
                pl.when(n_valid > s * sub)(compute)

    @pl.when(active & (j == last_j))
    def _():
        base = i * tm
        n_valid = rows_ref[i]

        def issue(r, _):
            pltpu.make_async_copy(acc_ref.at[pl.ds(r, 1)], ys_hbm.at[pl.ds(dst_ref[base + r], 1)], sem_out).start()
            return 0

        def issue8(b, _):
            for u in range(8):
                issue(b * 8 + u, 0)
            return 0

        n8 = lax.shift_right_logical(n_valid, 3)
        lax.fori_loop(0, n8, issue8, 0)
        lax.fori_loop(n8 * 8, n_valid, issue, 0)

        @pl.when(i == n_active - 1)
        def _():
            wait_scattered(n_valid)


def _moe_experts(h, tables, w1, w3, w2, layer, *, tm, sub=512, tf=256):
    t, d = h.shape
    f = w1.shape[3]
    nf = f // tf
    assert nf >= 2 and tm % sub == 0
    tile_expert, n_active, row_token, row_dst, tile_rows, row_gate = tables
    n_tiles = tile_expert.shape[0]
    kern = functools.partial(_moe_kernel, tm=tm, sub=sub)

    def f_idx(i, j, na):
        return jnp.where(i < na[0], j, nf - 1)

    grid_spec = pltpu.PrefetchScalarGridSpec(
        num_scalar_prefetch=5,
        grid=(n_tiles, nf),
        in_specs=[pl.BlockSpec(memory_space=pl.ANY),
                  pl.BlockSpec((tm, 1), lambda i, j, te, na, *_: (i, 0)),
                  pl.BlockSpec((None, None, d, tf), lambda i, j, te, na, *_: (layer, te[i], 0, f_idx(i, j, na))),
                  pl.BlockSpec((None, None, d, tf), lambda i, j, te, na, *_: (layer, te[i], 0, f_idx(i, j, na))),
                  pl.BlockSpec((None, None, tf, d), lambda i, j, te, na, *_: (layer, te[i], f_idx(i, j, na), 0))],
        out_specs=pl.BlockSpec(memory_space=pl.ANY),
        scratch_shapes=[pltpu.VMEM((tm, d), F32), pltpu.VMEM((tm, d), BF16), pltpu.VMEM((tm, d), F32),
                        pltpu.SemaphoreType.DMA(()), pltpu.SemaphoreType.DMA(())],
    )
    return pl.pallas_call(
        kern,
        grid_spec=grid_spec,
        out_shape=jax.ShapeDtypeStruct((TOP_K * t, d), F32),
        compiler_params=_params(("arbitrary", "arbitrary"), 56),
        name="moe_experts",
    )(tile_expert, n_active, row_token, row_dst, tile_rows, h, row_gate, w1, w3, w2)


def _combine_kernel(x_ref, y0_ref, y1_ref, mod_ref, out_ref):
    out_ref[...] = x_ref[...] + mod_ref[5:6, :] * (y0_ref[...] + y1_ref[...])


def _combine(x, ys, mod, *, seq, tm=512):
    t, d = x.shape
    nt = t // tm
    tiles_per_batch = seq // tm
    return pl.pallas_call(
        _combine_kernel,
        grid=(nt,),
        in_specs=[pl.BlockSpec((tm, d), lambda i: (i, 0)),
                  pl.BlockSpec((tm, d), lambda i: (i, 0)),
                  pl.BlockSpec((tm, d), lambda i: (nt + i, 0)),
                  pl.BlockSpec((None, 6, d), lambda i: (i // tiles_per_batch, 0, 0))],
        out_specs=pl.BlockSpec((tm, d), lambda i: (i, 0)),
        out_shape=jax.ShapeDtypeStruct((t, d), F32),
        compiler_params=_params(("arbitrary",), 48),
        name="moe_combine",
    )(x, ys, ys, mod)


def _final_norm_kernel(x_ref, g_ref, o_ref):
    x = x_ref[...]
    o_ref[...] = x * lax.rsqrt(jnp.mean(x * x, axis=-1, keepdims=True) + EPS) * g_ref[...]


def _final_norm(x, g, tm=512):
    t, d = x.shape
    return pl.pallas_call(
        _final_norm_kernel,
        grid=(t // tm,),
        in_specs=[pl.BlockSpec((tm, d), lambda i: (i, 0)), pl.BlockSpec((1, d), lambda i: (0, 0))],
        out_specs=pl.BlockSpec((tm, d), lambda i: (i, 0)),
        out_shape=jax.ShapeDtypeStruct((t, d), F32),
        compiler_params=_params(("arbitrary",), 32),
        name="final_norm",
    )(x, g)


def kernel(x, c, ada_w, ada_b, norm_mix_g, norm_ffn_g, w_in, b_f, conv_w, conv_b, conv_ln_g, conv_ln_b,
           conv_out, attn_out, w_out, ffn_w1, ffn_w3, ffn_w2, router_w, router_b, moe_w1, moe_w3, moe_w2,
           final_norm_g):
    bsz, seq, d = x.shape
    depth = ada_w.shape[0]
    d_conv = conv_w.shape[2]
    d_attn = attn_out.shape[1]
    t = bsz * seq
    moe_tm = 1024

    mods = _adaln(c, ada_w, ada_b).reshape(depth, bsz, 6, d)
    xf = x.reshape(t, d)
    for l in range(depth):
        mod = mods[l]
        w_parts = _split_w_in(w_in[l], d_attn=d_attn, d_conv=d_conv)
        qkv, glu, gates, fg = _mix_in(xf, mod, norm_mix_g[l].reshape(1, d), w_parts, seq=seq)
        z = _conv_branch(glu, conv_w[l], conv_b[l], conv_ln_g[l], conv_ln_b[l], seq=seq)
        o = _fox_attn(qkv, _fox_gate(fg, b_f[l], seq=seq), seq=seq, d_attn=d_attn)
        xf = _mix_out(z, o, gates, conv_out[l].astype(BF16), attn_out[l].astype(BF16), w_out[l].astype(BF16),
                      xf, mod, seq=seq)
        g_ffn = norm_ffn_g[l].reshape(1, d)
        i = l // 2
        if l % 2 == 0:
            xf = _ffn(xf, mod, g_ffn, ffn_w1[i].astype(BF16), ffn_w3[i].astype(BF16), ffn_w2[i].astype(BF16),
                      seq=seq)
        else:
            h, idx, wts = _router(xf, mod, g_ffn, router_w[i], router_b[i], seq=seq)
            tables = _dispatch_tables(idx[:, :TOP_K], wts[:, :TOP_K], tm=moe_tm)
            ys = _moe_experts(h, tables, moe_w1, moe_w3, moe_w2, i, tm=moe_tm)
            xf = _combine(xf, ys, mod, seq=seq)
    return _final_norm(xf, final_norm_g.reshape(1, d)).reshape(bsz, seq, d)
```

```python
import functools

import jax
import jax.numpy as jnp
from jax import lax
from jax.experimental import pallas as pl
from jax.experimental.pallas import tpu as pltpu

EPS = 1e-6
NEG_INF = -1e30
LOG2E = 1.4426950408889634
N_HEADS = 16
HEAD_DIM = 64
CONV_K = 31
N_EXPERTS = 8
TOP_K = 2

LANES = 128
CONV_HALO = 32
CONV_ROWS = 64
MIB = 1024 * 1024

F32 = jnp.float32
BF16 = jnp.bfloat16


def _params(semantics, vmem_mib):
    return pltpu.CompilerParams(dimension_semantics=semantics, vmem_limit_bytes=vmem_mib * MIB)


def _dot(a, b):
    return jnp.dot(a, b, preferred_element_type=F32)


def _silu(x):
    return x * jax.nn.sigmoid(x)


def _norm_mod(x, g, shift, scale):
    y = x * lax.rsqrt(jnp.mean(x * x, axis=-1, keepdims=True) + EPS) * g
    return y * (1.0 + scale) + shift


def _adaln_kernel(ct_ref, w_ref, b_ref, o_ref):
    w = w_ref[...]
    for b in range(o_ref.shape[0]):
        col = _silu(ct_ref[:, b:b + 1])
        o_ref[b:b + 1, :] = jnp.sum(w * col, axis=0, keepdims=True) + b_ref[...]


def _adaln(c, ada_w, ada_b, tn=1024):
    depth, d, n = ada_w.shape
    bsz = c.shape[0]
    return pl.pallas_call(
        _adaln_kernel,
        grid=(depth, n // tn),
        in_specs=[pl.BlockSpec((d, bsz), lambda l, j: (0, 0)),
                  pl.BlockSpec((None, d, tn), lambda l, j: (l, 0, j)),
                  pl.BlockSpec((None, 1, tn), lambda l, j: (l, 0, j))],
        out_specs=pl.BlockSpec((None, bsz, tn), lambda l, j: (l, 0, j)),
        out_shape=jax.ShapeDtypeStruct((depth, bsz, n), F32),
        compiler_params=_params(("arbitrary", "arbitrary"), 40),
        name="adaln",
    )(c.T, ada_w, ada_b.reshape(depth, 1, n))


def _mix_in_kernel(x_ref, mod_ref, g_ref, wqkv_ref, wa_ref, wg_ref, wgate_ref, wfg_ref,
                   qkv_ref, glu_ref, gate_ref, fg_ref, h_ref, *, n_q, n_qkv, n_glu):
    j = pl.program_id(1)

    @pl.when(j == 0)
    def _():
        h = _norm_mod(x_ref[...], g_ref[...], mod_ref[0:1, :], mod_ref[1:2, :]).astype(BF16)
        h_ref[...] = h
        fg_ref[...] = _dot(h, wfg_ref[...])

    @pl.when(j < n_q)
    def _():
        qkv_ref[...] = (_dot(h_ref[...], wqkv_ref[...]) * (HEAD_DIM ** -0.5 * LOG2E)).astype(BF16)

    @pl.when((j >= n_q) & (j < n_qkv))
    def _():
        qkv_ref[...] = _dot(h_ref[...], wqkv_ref[...]).astype(BF16)

    @pl.when((j >= n_qkv) & (j < n_qkv + n_glu))
    def _():
        h = h_ref[...]
        glu_ref[...] = (_dot(h, wa_ref[...]) * jax.nn.sigmoid(_dot(h, wg_ref[...]))).astype(BF16)

    @pl.when(j >= n_qkv + n_glu)
    def _():
        gate_ref[...] = jax.nn.sigmoid(_dot(h_ref[...], wgate_ref[...])).astype(BF16)


def _mix_in(x, mod, g, w_parts, *, seq, tm=1024, tn=512):
    t, d = x.shape
    w_qkv, w_a, w_g, w_gate, w_fg = w_parts
    d_attn, d_conv = w_qkv.shape[1] // 3, w_a.shape[1]
    n_q, n_qkv, n_glu, n_gate = d_attn // tn, 3 * d_attn // tn, 2 * d_conv // tn, 2 * d // tn
    tiles_per_batch = seq // tm
    glu_idx = lambda j: jnp.clip(j - n_qkv, 0, n_glu - 1)
    gate_idx = lambda j: jnp.clip(j - n_qkv - n_glu, 0, n_gate - 1)
    kern = functools.partial(_mix_in_kernel, n_q=n_q, n_qkv=n_qkv, n_glu=n_glu)
    return pl.pallas_call(
        kern,
        grid=(t // tm, n_qkv + n_glu + n_gate),
        in_specs=[pl.BlockSpec((tm, d), lambda i, j: (i, 0)),
                  pl.BlockSpec((None, 6, d), lambda i, j: (i // tiles_per_batch, 0, 0)),
                  pl.BlockSpec((1, d), lambda i, j: (0, 0)),
                  pl.BlockSpec((d, tn), lambda i, j: (0, jnp.minimum(j, n_qkv - 1))),
                  pl.BlockSpec((d, tn // 2), lambda i, j: (0, glu_idx(j))),
                  pl.BlockSpec((d, tn // 2), lambda i, j: (0, glu_idx(j))),
                  pl.BlockSpec((d, tn), lambda i, j: (0, gate_idx(j))),
                  pl.BlockSpec((d, LANES), lambda i, j: (0, 0))],
        out_specs=[pl.BlockSpec((tm, tn), lambda i, j: (i, jnp.minimum(j, n_qkv - 1))),
                   pl.BlockSpec((tm, tn // 2), lambda i, j: (i, glu_idx(j))),
                   pl.BlockSpec((tm, tn), lambda i, j: (i, gate_idx(j))),
                   pl.BlockSpec((tm, LANES), lambda i, j: (i, 0))],
        out_shape=[jax.ShapeDtypeStruct((t, 3 * d_attn), BF16),
                   jax.ShapeDtypeStruct((t, d_conv), BF16),
                   jax.ShapeDtypeStruct((t, 2 * d), BF16),
                   jax.ShapeDtypeStruct((t, LANES), F32)],
        scratch_shapes=[pltpu.VMEM((tm, d), BF16)],
        compiler_params=_params(("arbitrary", "arbitrary"), 48),
        name="mix_in",
    )(x, mod, g, w_qkv, w_a, w_g, w_gate, w_fg)


def _split_w_in(w, *, d_attn, d_conv):
    o = 3 * d_attn
    w_fg = jnp.pad(w[:, o:o + N_HEADS].astype(BF16), ((0, 0), (0, LANES - N_HEADS)))
    u = o + N_HEADS
    g = u + 2 * d_conv
    return (w[:, :o].astype(BF16), w[:, u:u + d_conv].astype(BF16), w[:, u + d_conv:g].astype(BF16),
            w[:, g:].astype(BF16), w_fg)


def _fox_gate_kernel(fg_ref, bf_ref, o_ref, carry_ref):
    @pl.when(pl.program_id(1) == 0)
    def _():
        carry_ref[...] = jnp.zeros_like(carry_ref)

    x = fg_ref[...] + bf_ref[...]
    logf = jnp.minimum(x, 0.0) - jnp.log(1.0 + jnp.exp(-jnp.abs(x)))
    n = x.shape[0]
    tri = (lax.broadcasted_iota(jnp.int32, (n, n), 1) <= lax.broadcasted_iota(jnp.int32, (n, n), 0)).astype(BF16)
    hi = logf.astype(BF16)
    r1 = logf - hi.astype(F32)
    mid = r1.astype(BF16)
    lo = (r1 - mid.astype(F32)).astype(BF16)
    cs = _dot(tri, hi) + _dot(tri, mid) + _dot(tri, lo) + carry_ref[...]
    o_ref[...] = cs
    carry_ref[...] = cs[n - 1:n, :]


def _fox_gate(fg, b_f, *, seq, chunk=256):
    t = fg.shape[0]
    bsz = t // seq
    bf = jnp.pad(b_f.astype(F32), (0, LANES - b_f.shape[0])).reshape(1, LANES)
    return pl.pallas_call(
        _fox_gate_kernel,
        grid=(bsz, seq // chunk),
        in_specs=[pl.BlockSpec((chunk, LANES), lambda b, i: (b * (seq // chunk) + i, 0)),
                  pl.BlockSpec((1, LANES), lambda b, i: (0, 0))],
        out_specs=pl.BlockSpec((chunk, LANES), lambda b, i: (b * (seq // chunk) + i, 0)),
        out_shape=jax.ShapeDtypeStruct((t, LANES), F32),
        scratch_shapes=[pltpu.VMEM((1, LANES), F32)],
        compiler_params=_params(("arbitrary", "arbitrary"), 16),
        name="fox_gate",
    )(fg, bf)


def _split3(x):
    hi = x.astype(BF16)
    r = x - hi.astype(F32)
    mid = r.astype(BF16)
    return hi, mid, (r - mid.astype(F32)).astype(BF16)


def _fox_attn_kernel(q_ref, k_ref, v_ref, f_ref, o_ref, kx_ref, vx_ref, *, tq):
    g = pl.program_id(1)
    qi = pl.program_id(2)
    n_heads = kx_ref.shape[0]
    lane = lax.broadcasted_iota(jnp.int32, (1, LANES), 1)
    first = lane < HEAD_DIM
    data = (first, jnp.logical_not(first))
    spare = (HEAD_DIM, 0)

    def pair_lanes(t):
        return slice((t // 2) * LANES, (t // 2 + 1) * LANES)

    @pl.when(qi == 0)
    def _():
        f_all = f_ref[...]
        for t in range(n_heads):
            k = k_ref[:, pair_lanes(t)].astype(F32)
            v = v_ref[:, pair_lanes(t)].astype(F32)
            f_key = jnp.sum(jnp.where(lane == n_heads * g + t, f_all, 0.0), axis=1, keepdims=True) * (-LOG2E)
            hi, mid, lo = (piece.astype(F32) for piece in _split3(f_key))
            s0 = spare[t % 2]
            bias = jnp.where(lane == s0, hi, jnp.where(lane == s0 + 1, mid, jnp.where(lane == s0 + 2, lo, 0.0)))
            kx_ref[t] = jnp.where(data[t % 2], k, bias).astype(BF16)
            vx_ref[t] = jnp.where(data[t % 2], v, jnp.where(lane == s0, 1.0, 0.0)).astype(BF16)

    f_rows = f_ref[pl.ds(pl.multiple_of(qi * tq, tq), tq), :]
    q_h, fq_h = [], []
    for t in range(n_heads):
        q = q_ref[:, pair_lanes(t)].astype(F32)
        s0 = spare[t % 2]
        ones = (lane >= s0) & (lane < s0 + 3)
        q_h.append(jnp.where(data[t % 2], q, jnp.where(ones, 1.0, 0.0)).astype(BF16))
        fq_h.append(jnp.sum(jnp.where(lane == n_heads * g + t, f_rows, 0.0), axis=1, keepdims=True) * LOG2E)
    causal = lax.broadcasted_iota(jnp.int32, (tq, tq), 1) <= lax.broadcasted_iota(jnp.int32, (tq, tq), 0)

    def step(j, carry, masked):
        off = pl.multiple_of(j * tq, tq)
        out = []
        for t in range(n_heads):
            m, acc = carry[t]
            s = lax.dot_general(q_h[t], kx_ref[t, pl.ds(off, tq), :], (((1,), (1,)), ((), ())),
                                preferred_element_type=F32)
            if masked:
                s = jnp.where(causal, s, NEG_INF)
            m_new = jnp.maximum(m, jnp.max(s, axis=1, keepdims=True) + fq_h[t])
            pexp = jnp.exp2(s + (fq_h[t] - m_new))
            acc = jnp.exp2(m - m_new) * acc + _dot(pexp.astype(BF16), vx_ref[t, pl.ds(off, tq), :])
            out.append((m_new, acc))
        return tuple(out)

    init = tuple((jnp.full((tq, 1), NEG_INF, F32), jnp.zeros((tq, LANES), F32)) for _ in range(n_heads))
    carry = lax.fori_loop(0, qi, lambda j, c: step(j, c, False), init)
    accs = [acc for _, acc in step(qi, carry, True)]
    for t in range(0, n_heads, 2):
        out_a = accs[t] / accs[t][:, spare[0]:spare[0] + 1]
        out_b = accs[t + 1] / accs[t + 1][:, spare[1]:spare[1] + 1]
        o_ref[:, pair_lanes(t)] = jnp.where(first, out_a, out_b).astype(o_ref.dtype)


def _fox_attn(qkv, f_cum, *, seq, d_attn, tq=512, heads=4):
    t = qkv.shape[0]
    bsz = t // seq
    width = heads * HEAD_DIM
    n_groups = d_attn // width
    nq = seq // tq
    kern = functools.partial(_fox_attn_kernel, tq=tq)
    return pl.pallas_call(
        kern,
        grid=(bsz, n_groups, nq),
        in_specs=[pl.BlockSpec((tq, width), lambda b, g, i: (b * nq + i, g)),
                  pl.BlockSpec((seq, width), lambda b, g, i: (b, n_groups + g)),
                  pl.BlockSpec((seq, width), lambda b, g, i: (b, 2 * n_groups + g)),
                  pl.BlockSpec((seq, LANES), lambda b, g, i: (b, 0))],
        out_specs=pl.BlockSpec((tq, width), lambda b, g, i: (b * nq + i, g)),
        out_shape=jax.ShapeDtypeStruct((t, d_attn), BF16),
        scratch_shapes=[pltpu.VMEM((heads, seq, LANES), BF16), pltpu.VMEM((heads, seq, LANES), BF16)],
        compiler_params=_params(("arbitrary", "arbitrary", "arbitrary"), 48),
        name="fox_attn",
    )(qkv, qkv, qkv, f_cum)


def _conv_kernel(cur_ref, halo_ref, w_ref, cb_ref, lg_ref, lb_ref, o_ref, ybuf, cbuf, *, ts):
    qi = pl.program_id(1)
    c = cur_ref.shape[1]
    halo = halo_ref[...].astype(F32)
    ybuf[0:CONV_HALO, :] = jnp.where(qi > 0, halo, 0.0)
    ybuf[CONV_HALO:CONV_HALO + ts, :] = cur_ref[...].astype(F32)

    def lane_chunk(ci, _):
        lo = pl.multiple_of(ci * LANES, LANES)
        for r in range(ts // CONV_ROWS):
            acc = jnp.broadcast_to(cb_ref[:, pl.ds(lo, LANES)], (CONV_ROWS, LANES))
            for j in range(CONV_K):
                start = r * CONV_ROWS + CONV_HALO - (CONV_K - 1) + j
                acc = acc + w_ref[j:j + 1, pl.ds(lo, LANES)] * ybuf[start:start + CONV_ROWS, pl.ds(lo, LANES)]
            cbuf[r * CONV_ROWS:(r + 1) * CONV_ROWS, pl.ds(lo, LANES)] = acc
        return 0

    lax.fori_loop(0, c // LANES, lane_chunk, 0)
    y = cbuf[...]
    mu = jnp.mean(y, axis=-1, keepdims=True)
    yc = y - mu
    z = yc * lax.rsqrt(jnp.mean(yc * yc, axis=-1, keepdims=True) + EPS) * lg_ref[...] + lb_ref[...]
    o_ref[...] = _silu(z).astype(o_ref.dtype)


def _conv_branch(y, conv_w, conv_b, ln_g, ln_b, *, seq, ts=256):
    t, c = y.shape
    bsz = t // seq
    ns = seq // ts
    w = jnp.pad(conv_w.astype(F32), ((0, CONV_HALO - CONV_K), (0, 0)))
    kern = functools.partial(_conv_kernel, ts=ts)
    row = lambda a: a.astype(F32).reshape(1, c)
    return pl.pallas_call(
        kern,
        grid=(bsz, ns),
        in_specs=[pl.BlockSpec((ts, c), lambda b, i: (b * ns + i, 0)),
                  pl.BlockSpec((CONV_HALO, c),
                               lambda b, i: (jnp.maximum((b * ns + i) * (ts // CONV_HALO) - 1, 0), 0)),
                  pl.BlockSpec((CONV_HALO, c), lambda b, i: (0, 0)),
                  pl.BlockSpec((1, c), lambda b, i: (0, 0)),
                  pl.BlockSpec((1, c), lambda b, i: (0, 0)),
                  pl.BlockSpec((1, c), lambda b, i: (0, 0))],
        out_specs=pl.BlockSpec((ts, c), lambda b, i: (b * ns + i, 0)),
        out_shape=jax.ShapeDtypeStruct((t, c), BF16),
        scratch_shapes=[pltpu.VMEM((CONV_HALO + ts, c), F32), pltpu.VMEM((ts, c), F32)],
        compiler_params=_params(("arbitrary", "arbitrary"), 32),
        name="conv_branch",
    )(y, y, w, row(conv_b), row(ln_g), row(ln_b))


def _mix_out_kernel(z_ref, o_ref, sga_ref, sgb_ref, co_ref, ao_ref, wo_ref, x_ref, mod_ref, out_ref, acc_ref):
    j = pl.program_id(1)
    y_a = _dot(z_ref[...], co_ref[...])
    y_b = _dot(o_ref[...], ao_ref[...])
    merged = (sga_ref[...].astype(F32) * y_a + sgb_ref[...].astype(F32) * y_b).astype(BF16)

    @pl.when(j == 0)
    def _():
        acc_ref[...] = jnp.zeros_like(acc_ref)

    acc_ref[...] += _dot(merged, wo_ref[...])

    @pl.when(j == pl.num_programs(1) - 1)
    def _():
        out_ref[...] = x_ref[...] + mod_ref[2:3, :] * acc_ref[...]


def _mix_out(z, o, gates, conv_out, attn_out, w_out, x, mod, *, seq, tm=512, tn=1024):
    t, d = x.shape
    c = z.shape[1]
    nj = d // tn
    tiles_per_batch = seq // tm
    return pl.pallas_call(
        _mix_out_kernel,
        grid=(t // tm, nj),
        in_specs=[pl.BlockSpec((tm, c), lambda i, j: (i, 0)),
                  pl.BlockSpec((tm, c), lambda i, j: (i, 0)),
                  pl.BlockSpec((tm, tn), lambda i, j: (i, j)),
                  pl.BlockSpec((tm, tn), lambda i, j: (i, nj + j)),
                  pl.BlockSpec((c, tn), lambda i, j: (0, j)),
                  pl.BlockSpec((c, tn), lambda i, j: (0, j)),
                  pl.BlockSpec((tn, d), lambda i, j: (j, 0)),
                  pl.BlockSpec((tm, d), lambda i, j: (i, 0)),
                  pl.BlockSpec((None, 6, d), lambda i, j: (i // tiles_per_batch, 0, 0))],
        out_specs=pl.BlockSpec((tm, d), lambda i, j: (i, 0)),
        out_shape=jax.ShapeDtypeStruct((t, d), F32),
        scratch_shapes=[pltpu.VMEM((tm, d), F32)],
        compiler_params=_params(("arbitrary", "arbitrary"), 48),
        name="mix_out",
    )(z, o, gates, gates, conv_out, attn_out, w_out, x, mod)


def _ffn_kernel(x_ref, mod_ref, g_ref, w1_ref, w3_ref, w2_ref, out_ref, h_ref, acc_ref):
    j = pl.program_id(1)

    @pl.when(j == 0)
    def _():
        h_ref[...] = _norm_mod(x_ref[...], g_ref[...], mod_ref[3:4, :], mod_ref[4:5, :]).astype(BF16)
        acc_ref[...] = jnp.zeros_like(acc_ref)

    h = h_ref[...]
    hidden = (_silu(_dot(h, w1_ref[...])) * _dot(h, w3_ref[...])).astype(BF16)
    acc_ref[...] += _dot(hidden, w2_ref[...])

    @pl.when(j == pl.num_programs(1) - 1)
    def _():
        out_ref[...] = x_ref[...] + mod_ref[5:6, :] * acc_ref[...]


def _ffn(x, mod, g, w1, w3, w2, *, seq, tm=512, tf=512):
    t, d = x.shape
    f = w1.shape[1]
    tiles_per_batch = seq // tm
    return pl.pallas_call(
        _ffn_kernel,
        grid=(t // tm, f // tf),
        in_specs=[pl.BlockSpec((tm, d), lambda i, j: (i, 0)),
                  pl.BlockSpec((None, 6, d), lambda i, j: (i // tiles_per_batch, 0, 0)),
                  pl.BlockSpec((1, d), lambda i, j: (0, 0)),
                  pl.BlockSpec((d, tf), lambda i, j: (0, j)),
                  pl.BlockSpec((d, tf), lambda i, j: (0, j)),
                  pl.BlockSpec((tf, d), lambda i, j: (j, 0))],
        out_specs=pl.BlockSpec((tm, d), lambda i, j: (i, 0)),
        out_shape=jax.ShapeDtypeStruct((t, d), F32),
        scratch_shapes=[pltpu.VMEM((tm, d), BF16), pltpu.VMEM((tm, d), F32)],
        compiler_params=_params(("arbitrary", "arbitrary"), 48),
        name="ffn",
    )(x, mod, g, w1, w3, w2)


def _router_kernel(x_ref, mod_ref, g_ref, rw_ref, rb_ref, h_ref, idx_ref, wt_ref):
    h = _norm_mod(x_ref[...], g_ref[...], mod_ref[3:4, :], mod_ref[4:5, :])
    h_ref[...] = h
    logits = jnp.dot(h, rw_ref[...], preferred_element_type=F32, precision=lax.Precision.HIGHEST) + rb_ref[...]
    lane = lax.broadcasted_iota(jnp.int32, logits.shape, 1).astype(F32)
    lg = jnp.where(lane < N_EXPERTS, logits, -jnp.inf)
    v1 = jnp.max(lg, axis=1, keepdims=True)
    i1 = jnp.min(jnp.where(lg == v1, lane, float(LANES)), axis=1, keepdims=True)
    lg2 = jnp.where(lane == i1, -jnp.inf, lg)
    v2 = jnp.max(lg2, axis=1, keepdims=True)
    i2 = jnp.min(jnp.where(lg2 == v2, lane, float(LANES)), axis=1, keepdims=True)
    e2 = jnp.exp(v2 - v1)
    w1 = 1.0 / (1.0 + e2)
    w2 = e2 / (1.0 + e2)
    idx_ref[...] = jnp.where(lane == 0.0, i1, jnp.where(lane == 1.0, i2, 0.0)).astype(jnp.int32)
    wt_ref[...] = jnp.where(lane == 0.0, w1, jnp.where(lane == 1.0, w2, 0.0))


def _router(x, mod, g, router_w, router_b, *, seq, tm=512):
    t, d = x.shape
    tiles_per_batch = seq // tm
    rw = jnp.pad(router_w.astype(F32), ((0, 0), (0, LANES - N_EXPERTS)))
    rb = jnp.pad(router_b.astype(F32), (0, LANES - N_EXPERTS)).reshape(1, LANES)
    return pl.pallas_call(
        _router_kernel,
        grid=(t // tm,),
        in_specs=[pl.BlockSpec((tm, d), lambda i: (i, 0)),
                  pl.BlockSpec((None, 6, d), lambda i: (i // tiles_per_batch, 0, 0)),
                  pl.BlockSpec((1, d), lambda i: (0, 0)),
                  pl.BlockSpec((d, LANES), lambda i: (0, 0)),
                  pl.BlockSpec((1, LANES), lambda i: (0, 0))],
        out_specs=[pl.BlockSpec((tm, d), lambda i: (i, 0)),
                   pl.BlockSpec((tm, LANES), lambda i: (i, 0)),
                   pl.BlockSpec((tm, LANES), lambda i: (i, 0))],
        out_shape=[jax.ShapeDtypeStruct((t, d), F32),
                   jax.ShapeDtypeStruct((t, LANES), jnp.int32),
                   jax.ShapeDtypeStruct((t, LANES), F32)],
        compiler_params=_params(("arbitrary",), 40),
        name="router",
    )(x, mod, g, rw, rb)


def _dispatch_tables(idx, wts, *, tm):
    t = idx.shape[0]
    n_rows = TOP_K * t + N_EXPERTS * tm
    n_tiles = n_rows // tm
    flat_e = idx.reshape(-1)
    onehot = (flat_e[:, None] == jnp.arange(N_EXPERTS, dtype=jnp.int32)[None, :]).astype(jnp.int32)
    rank = jnp.sum((jnp.cumsum(onehot, axis=0) - onehot) * onehot, axis=1)
    counts = jnp.sum(onehot, axis=0)
    padded = ((counts + tm - 1) // tm) * tm
    ends = jnp.cumsum(padded)
    starts = ends - padded
    pos = starts[flat_e] + rank
    row_src = jnp.full((n_rows,), -1, jnp.int32).at[pos].set(jnp.arange(TOP_K * t, dtype=jnp.int32))
    valid = row_src >= 0
    src = jnp.maximum(row_src, 0)
    row_token = src // TOP_K
    row_dst = jnp.where(valid, (src % TOP_K) * t + row_token, -1)
    row_gate = jnp.where(valid, wts.reshape(-1)[src], 0.0).reshape(n_rows, 1)
    n_active = (ends[-1] // tm).astype(jnp.int32)
    tile_start = jnp.minimum(jnp.arange(n_tiles, dtype=jnp.int32), n_active - 1) * tm
    tile_expert = jnp.minimum(jnp.sum((tile_start[:, None] >= ends[None, :]).astype(jnp.int32), axis=1),
                              N_EXPERTS - 1)
    tile_rows = jnp.sum(valid.reshape(n_tiles, tm).astype(jnp.int32), axis=1)
    return tile_expert, n_active.reshape(1), row_token, row_dst, tile_rows, row_gate


def _moe_kernel(te_ref, na_ref, tok_ref, dst_ref, rows_ref, h_hbm, gate_ref, w1_ref, w3_ref, w2_ref, ys_hbm,
                xbuf, xbf, acc_ref, sem_in, sem_out, *, tm, sub):
    i = pl.program_id(0)
    j = pl.program_id(1)
    last_j = pl.num_programs(1) - 1
    n_active = na_ref[0]
    active = i < n_active

    def gather_start(tile):
        base = tile * tm

        def issue(r, _):
            pltpu.make_async_copy(h_hbm.at[pl.ds(tok_ref[base + r], 1)], xbuf.at[pl.ds(r, 1)], sem_in).start()
            return 0

        lax.fori_loop(0, tm, issue, 0, unroll=8)

    def wait_scattered(n):
        n8 = lax.shift_right_logical(n, 3)

        @pl.when(n8 > 0)
        def _():
            rows = pl.ds(0, pl.multiple_of(n8 * 8, 8))
            pltpu.make_async_copy(acc_ref.at[rows], ys_hbm.at[rows], sem_out).wait()

        def wait_row(r, _):
            pltpu.make_async_copy(acc_ref.at[pl.ds(0, 1)], ys_hbm.at[pl.ds(0, 1)], sem_out).wait()
            return 0

        lax.fori_loop(n8 * 8, n, wait_row, 0)

    @pl.when(active & (j == 0))
    def _():
        @pl.when(i == 0)
        def _():
            gather_start(0)

        pltpu.make_async_copy(h_hbm.at[pl.ds(0, tm)], xbuf, sem_in).wait()
        xbf[...] = xbuf[...].astype(BF16)

        @pl.when(i + 1 < n_active)
        def _():
            gather_start(i + 1)

        @pl.when(i > 0)
        def _():
            wait_scattered(rows_ref[i - 1])

    @pl.when(active)
    def _():
        w1 = w1_ref[...].astype(BF16)
        w3 = w3_ref[...].astype(BF16)
        w2 = w2_ref[...].astype(BF16)
        n_valid = rows_ref[i]
        for s in range(tm // sub):
            rows = slice(s * sub, (s + 1) * sub)

            def compute(rows=rows):
                @pl.when(j == 0)
                def _():
                    acc_ref[rows, :] = jnp.zeros((sub, acc_ref.shape[1]), F32)

                x = xbf[rows, :]
                hidden = (_silu(_dot(x, w1)) * _dot(x, w3)).astype(BF16)
                acc_ref[rows, :] += _dot(hidden, w2)

                @pl.when(j == last_j)
                def _():
                    acc_ref[rows, :] = acc_ref[rows, :] * gate_ref[rows, :]

            if s == 0:
                compute()
            else:
                pl.when(n_valid > s * sub)(compute)

    @pl.when(active & (j == last_j))
    def _():
        base = i * tm
        n_valid = rows_ref[i]

        def issue(r, _):
            pltpu.make_async_copy(acc_ref.at[pl.ds(r, 1)], ys_hbm.at[pl.ds(dst_ref[base + r], 1)], sem_out).start()
            return 0

        def issue8(b, _):
            for u in range(8):
                issue(b * 8 + u, 0)
            return 0

        n8 = lax.shift_right_logical(n_valid, 3)
        lax.fori_loop(0, n8, issue8, 0)
        lax.fori_loop(n8 * 8, n_valid, issue, 0)

        @pl.when(i == n_active - 1)
        def _():
            wait_scattered(n_valid)


def _moe_experts(h, tables, w1, w3, w2, layer, *, tm, sub=512, tf=256):
    t, d = h.shape
    f = w1.shape[3]
    nf = f // tf
    assert nf >= 2 and tm % sub == 0
    tile_expert, n_active, row_token, row_dst, tile_rows, row_gate = tables
    n_tiles = tile_expert.shape[0]
    kern = functools.partial(_moe_kernel, tm=tm, sub=sub)

    def f_idx(i, j, na):
        return jnp.where(i < na[0], j, nf - 1)

    grid_spec = pltpu.PrefetchScalarGridSpec(
        num_scalar_prefetch=5,
        grid=(n_tiles, nf),
        in_specs=[pl.BlockSpec(memory_space=pl.ANY),
                  pl.BlockSpec((tm, 1), lambda i, j, te, na, *_: (i, 0)),
                  pl.BlockSpec((None, None, d, tf), lambda i, j, te, na, *_: (layer, te[i], 0, f_idx(i, j, na))),
                  pl.BlockSpec((None, None, d, tf), lambda i, j, te, na, *_: (layer, te[i], 0, f_idx(i, j, na))),
                  pl.BlockSpec((None, None, tf, d), lambda i, j, te, na, *_: (layer, te[i], f_idx(i, j, na), 0))],
        out_specs=pl.BlockSpec(memory_space=pl.ANY),
        scratch_shapes=[pltpu.VMEM((tm, d), F32), pltpu.VMEM((tm, d), BF16), pltpu.VMEM((tm, d), F32),
                        pltpu.SemaphoreType.DMA(()), pltpu.SemaphoreType.DMA(())],
    )
    return pl.pallas_call(
        kern,
        grid_spec=grid_spec,
        out_shape=jax.ShapeDtypeStruct((TOP_K * t, d), F32),
        compiler_params=_params(("arbitrary", "arbitrary"), 56),
        name="moe_experts",
    )(tile_expert, n_active, row_token, row_dst, tile_rows, h, row_gate, w1, w3, w2)


def _combine_kernel(x_ref, y0_ref, y1_ref, mod_ref, g_ref, out_ref, *, final):
    x = x_ref[...] + mod_ref[5:6, :] * (y0_ref[...] + y1_ref[...])
    if final:
        x = x * lax.rsqrt(jnp.mean(x * x, axis=-1, keepdims=True) + EPS) * g_ref[...]
    out_ref[...] = x


def _combine(x, ys, mod, final_g, *, seq, final, tm=512):
    t, d = x.shape
    nt = t // tm
    tiles_per_batch = seq // tm
    return pl.pallas_call(
        functools.partial(_combine_kernel, final=final),
        grid=(nt,),
        in_specs=[pl.BlockSpec((tm, d), lambda i: (i, 0)),
                  pl.BlockSpec((tm, d), lambda i: (i, 0)),
                  pl.BlockSpec((tm, d), lambda i: (nt + i, 0)),
                  pl.BlockSpec((None, 6, d), lambda i: (i // tiles_per_batch, 0, 0)),
                  pl.BlockSpec((1, d), lambda i: (0, 0))],
        out_specs=pl.BlockSpec((tm, d), lambda i: (i, 0)),
        out_shape=jax.ShapeDtypeStruct((t, d), F32),
        compiler_params=_params(("arbitrary",), 48),
        name="moe_combine",
    )(x, ys, ys, mod, final_g)


def _final_norm_kernel(x_ref, g_ref, o_ref):
    x = x_ref[...]
    o_ref[...] = x * lax.rsqrt(jnp.mean(x * x, axis=-1, keepdims=True) + EPS) * g_ref[...]


def _final_norm(x, g, tm=512):
    t, d = x.shape
    return pl.pallas_call(
        _final_norm_kernel,
        grid=(t // tm,),
        in_specs=[pl.BlockSpec((tm, d), lambda i: (i, 0)), pl.BlockSpec((1, d), lambda i: (0, 0))],
        out_specs=pl.BlockSpec((tm, d), lambda i: (i, 0)),
        out_shape=jax.ShapeDtypeStruct((t, d), F32),
        compiler_params=_params(("arbitrary",), 32),
        name="final_norm",
    )(x, g)


def kernel(x, c, ada_w, ada_b, norm_mix_g, norm_ffn_g, w_in, b_f, conv_w, conv_b, conv_ln_g, conv_ln_b,
           conv_out, attn_out, w_out, ffn_w1, ffn_w3, ffn_w2, router_w, router_b, moe_w1, moe_w3, moe_w2,
           final_norm_g):
    bsz, seq, d = x.shape
    depth = ada_w.shape[0]
    d_conv = conv_w.shape[2]
    d_attn = attn_out.shape[1]
    t = bsz * seq
    moe_tm = 1024
    final_g = final_norm_g.reshape(1, d)

    mods = _adaln(c, ada_w, ada_b).reshape(depth, bsz, 6, d)
    xf = x.reshape(t, d)
    for l in range(depth):
        mod = mods[l]
        w_parts = _split_w_in(w_in[l], d_attn=d_attn, d_conv=d_conv)
        qkv, glu, gates, fg = _mix_in(xf, mod, norm_mix_g[l].reshape(1, d), w_parts, seq=seq)
        z = _conv_branch(glu, conv_w[l], conv_b[l], conv_ln_g[l], conv_ln_b[l], seq=seq)
        o = _fox_attn(qkv, _fox_gate(fg, b_f[l], seq=seq), seq=seq, d_attn=d_attn)
        xf = _mix_out(z, o, gates, conv_out[l].astype(BF16), attn_out[l].astype(BF16), w_out[l].astype(BF16),
                      xf, mod, seq=seq)
        g_ffn = norm_ffn_g[l].reshape(1, d)
        i = l // 2
        if l % 2 == 0:
            xf = _ffn(xf, mod, g_ffn, ffn_w1[i].astype(BF16), ffn_w3[i].astype(BF16), ffn_w2[i].astype(BF16),
                      seq=seq)
        else:
            h, idx, wts = _router(xf, mod, g_ffn, router_w[i], router_b[i], seq=seq)
            tables = _dispatch_tables(idx[:, :TOP_K], wts[:, :TOP_K], tm=moe_tm)
            ys = _moe_experts(h, tables, moe_w1, moe_w3, moe_w2, i, tm=moe_tm)
            xf = _combine(xf, ys, mod, final_g, seq=seq, final=(l == depth - 1))
    if depth % 2 == 1:
        xf = _final_norm(xf, final_g)
    return xf.reshape(bsz, seq, d)
```

```python
import functools

import jax
import jax.numpy as jnp
from jax import lax
from jax.experimental import pallas as pl
from jax.experimental.pallas import tpu as pltpu

EPS = 1e-6
NEG_INF = -1e30
LOG2E = 1.4426950408889634
N_HEADS = 16
HEAD_DIM = 64
CONV_K = 31
N_EXPERTS = 8
TOP_K = 2

LANES = 128
CONV_HALO = 32
CONV_ROWS = 64
MIB = 1024 * 1024

F32 = jnp.float32
BF16 = jnp.bfloat16


def _params(semantics, vmem_mib):
    return pltpu.CompilerParams(dimension_semantics=semantics, vmem_limit_bytes=vmem_mib * MIB)


def _dot(a, b):
    return jnp.dot(a, b, preferred_element_type=F32)


def _silu(x):
    return x * jax.nn.sigmoid(x)


def _norm_mod(x, g, shift, scale):
    y = x * lax.rsqrt(jnp.mean(x * x, axis=-1, keepdims=True) + EPS) * g
    return y * (1.0 + scale) + shift


def _adaln_kernel(ct_ref, w_ref, b_ref, o_ref):
    w = w_ref[...]
    for b in range(o_ref.shape[0]):
        col = _silu(ct_ref[:, b:b + 1])
        o_ref[b:b + 1, :] = jnp.sum(w * col, axis=0, keepdims=True) + b_ref[...]


def _adaln(c, ada_w, ada_b, tn=1024):
    depth, d, n = ada_w.shape
    bsz = c.shape[0]
    return pl.pallas_call(
        _adaln_kernel,
        grid=(depth, n // tn),
        in_specs=[pl.BlockSpec((d, bsz), lambda l, j: (0, 0)),
                  pl.BlockSpec((None, d, tn), lambda l, j: (l, 0, j)),
                  pl.BlockSpec((None, 1, tn), lambda l, j: (l, 0, j))],
        out_specs=pl.BlockSpec((None, bsz, tn), lambda l, j: (l, 0, j)),
        out_shape=jax.ShapeDtypeStruct((depth, bsz, n), F32),
        compiler_params=_params(("arbitrary", "arbitrary"), 40),
        name="adaln",
    )(c.T, ada_w, ada_b.reshape(depth, 1, n))


def _mix_in_kernel(x_ref, mod_ref, g_ref, wqkv_ref, wa_ref, wg_ref, wgate_ref, wfg_ref,
                   qkv_ref, glu_ref, gate_ref, fg_ref, h_ref, *, n_q, n_qkv, n_glu):
    j = pl.program_id(1)

    @pl.when(j == 0)
    def _():
        h = _norm_mod(x_ref[...], g_ref[...], mod_ref[0:1, :], mod_ref[1:2, :]).astype(BF16)
        h_ref[...] = h
        fg_ref[...] = _dot(h, wfg_ref[...])

    @pl.when(j < n_q)
    def _():
        qkv_ref[...] = (_dot(h_ref[...], wqkv_ref[...]) * (HEAD_DIM ** -0.5 * LOG2E)).astype(BF16)

    @pl.when((j >= n_q) & (j < n_qkv))
    def _():
        qkv_ref[...] = _dot(h_ref[...], wqkv_ref[...]).astype(BF16)

    @pl.when((j >= n_qkv) & (j < n_qkv + n_glu))
    def _():
        h = h_ref[...]
        glu_ref[...] = (_dot(h, wa_ref[...]) * jax.nn.sigmoid(_dot(h, wg_ref[...]))).astype(BF16)

    @pl.when(j >= n_qkv + n_glu)
    def _():
        gate_ref[...] = jax.nn.sigmoid(_dot(h_ref[...], wgate_ref[...])).astype(BF16)


def _mix_in(x, mod, g, w_parts, *, seq, tm=1024, tn=512):
    t, d = x.shape
    w_qkv, w_a, w_g, w_gate, w_fg = w_parts
    d_attn, d_conv = w_qkv.shape[1] // 3, w_a.shape[1]
    n_q, n_qkv, n_glu, n_gate = d_attn // tn, 3 * d_attn // tn, 2 * d_conv // tn, 2 * d // tn
    tiles_per_batch = seq // tm
    glu_idx = lambda j: jnp.clip(j - n_qkv, 0, n_glu - 1)
    gate_idx = lambda j: jnp.clip(j - n_qkv - n_glu, 0, n_gate - 1)
    kern = functools.partial(_mix_in_kernel, n_q=n_q, n_qkv=n_qkv, n_glu=n_glu)
    return pl.pallas_call(
        kern,
        grid=(t // tm, n_qkv + n_glu + n_gate),
        in_specs=[pl.BlockSpec((tm, d), lambda i, j: (i, 0)),
                  pl.BlockSpec((None, 6, d), lambda i, j: (i // tiles_per_batch, 0, 0)),
                  pl.BlockSpec((1, d), lambda i, j: (0, 0)),
                  pl.BlockSpec((d, tn), lambda i, j: (0, jnp.minimum(j, n_qkv - 1))),
                  pl.BlockSpec((d, tn // 2), lambda i, j: (0, glu_idx(j))),
                  pl.BlockSpec((d, tn // 2), lambda i, j: (0, glu_idx(j))),
                  pl.BlockSpec((d, tn), lambda i, j: (0, gate_idx(j))),
                  pl.BlockSpec((d, LANES), lambda i, j: (0, 0))],
        out_specs=[pl.BlockSpec((tm, tn), lambda i, j: (i, jnp.minimum(j, n_qkv - 1))),
                   pl.BlockSpec((tm, tn // 2), lambda i, j: (i, glu_idx(j))),
                   pl.BlockSpec((tm, tn), lambda i, j: (i, gate_idx(j))),
                   pl.BlockSpec((tm, LANES), lambda i, j: (i, 0))],
        out_shape=[jax.ShapeDtypeStruct((t, 3 * d_attn), BF16),
                   jax.ShapeDtypeStruct((t, d_conv), BF16),
                   jax.ShapeDtypeStruct((t, 2 * d), BF16),
                   jax.ShapeDtypeStruct((t, LANES), F32)],
        scratch_shapes=[pltpu.VMEM((tm, d), BF16)],
        compiler_params=_params(("arbitrary", "arbitrary"), 48),
        name="mix_in",
    )(x, mod, g, w_qkv, w_a, w_g, w_gate, w_fg)


def _split_w_in(w, *, d_attn, d_conv):
    o = 3 * d_attn
    w_fg = jnp.pad(w[:, o:o + N_HEADS].astype(BF16), ((0, 0), (0, LANES - N_HEADS)))
    u = o + N_HEADS
    g = u + 2 * d_conv
    return (w[:, :o].astype(BF16), w[:, u:u + d_conv].astype(BF16), w[:, u + d_conv:g].astype(BF16),
            w[:, g:].astype(BF16), w_fg)


def _fox_gate_kernel(fg_ref, bf_ref, o_ref, carry_ref):
    @pl.when(pl.program_id(1) == 0)
    def _():
        carry_ref[...] = jnp.zeros_like(carry_ref)

    x = fg_ref[...] + bf_ref[...]
    logf = jnp.minimum(x, 0.0) - jnp.log(1.0 + jnp.exp(-jnp.abs(x)))
    n = x.shape[0]
    tri = (lax.broadcasted_iota(jnp.int32, (n, n), 1) <= lax.broadcasted_iota(jnp.int32, (n, n), 0)).astype(BF16)
    hi = logf.astype(BF16)
    r1 = logf - hi.astype(F32)
    mid = r1.astype(BF16)
    lo = (r1 - mid.astype(F32)).astype(BF16)
    cs = _dot(tri, hi) + _dot(tri, mid) + _dot(tri, lo) + carry_ref[...]
    o_ref[...] = cs
    carry_ref[...] = cs[n - 1:n, :]


def _fox_gate(fg, b_f, *, seq, chunk=256):
    t = fg.shape[0]
    bsz = t // seq
    bf = jnp.pad(b_f.astype(F32), (0, LANES - b_f.shape[0])).reshape(1, LANES)
    return pl.pallas_call(
        _fox_gate_kernel,
        grid=(bsz, seq // chunk),
        in_specs=[pl.BlockSpec((chunk, LANES), lambda b, i: (b * (seq // chunk) + i, 0)),
                  pl.BlockSpec((1, LANES), lambda b, i: (0, 0))],
        out_specs=pl.BlockSpec((chunk, LANES), lambda b, i: (b * (seq // chunk) + i, 0)),
        out_shape=jax.ShapeDtypeStruct((t, LANES), F32),
        scratch_shapes=[pltpu.VMEM((1, LANES), F32)],
        compiler_params=_params(("arbitrary", "arbitrary"), 16),
        name="fox_gate",
    )(fg, bf)


def _split3(x):
    hi = x.astype(BF16)
    r = x - hi.astype(F32)
    mid = r.astype(BF16)
    return hi, mid, (r - mid.astype(F32)).astype(BF16)


def _fox_attn_kernel(q_ref, k_ref, v_ref, f_ref, o_ref, kx_ref, vx_ref, *, tq):
    g = pl.program_id(1)
    qi = pl.program_id(2)
    n_heads = kx_ref.shape[0]
    lane = lax.broadcasted_iota(jnp.int32, (1, LANES), 1)
    first = lane < HEAD_DIM
    data = (first, jnp.logical_not(first))
    spare = (HEAD_DIM, 0)

    def pair_lanes(t):
        return slice((t // 2) * LANES, (t // 2 + 1) * LANES)

    @pl.when(qi == 0)
    def _():
        f_all = f_ref[...]
        for t in range(n_heads):
            k = k_ref[:, pair_lanes(t)].astype(F32)
            v = v_ref[:, pair_lanes(t)].astype(F32)
            f_key = jnp.sum(jnp.where(lane == n_heads * g + t, f_all, 0.0), axis=1, keepdims=True) * (-LOG2E)
            hi, mid, lo = (piece.astype(F32) for piece in _split3(f_key))
            s0 = spare[t % 2]
            bias = jnp.where(lane == s0, hi, jnp.where(lane == s0 + 1, mid, jnp.where(lane == s0 + 2, lo, 0.0)))
            kx_ref[t] = jnp.where(data[t % 2], k, bias).astype(BF16)
            vx_ref[t] = jnp.where(data[t % 2], v, jnp.where(lane == s0, 1.0, 0.0)).astype(BF16)

    f_rows = f_ref[pl.ds(pl.multiple_of(qi * tq, tq), tq), :]
    q_h, fq_h = [], []
    for t in range(n_heads):
        q = q_ref[:, pair_lanes(t)].astype(F32)
        s0 = spare[t % 2]
        ones = (lane >= s0) & (lane < s0 + 3)
        q_h.append(jnp.where(data[t % 2], q, jnp.where(ones, 1.0, 0.0)).astype(BF16))
        fq_h.append(jnp.sum(jnp.where(lane == n_heads * g + t, f_rows, 0.0), axis=1, keepdims=True) * LOG2E)
    causal = lax.broadcasted_iota(jnp.int32, (tq, tq), 1) <= lax.broadcasted_iota(jnp.int32, (tq, tq), 0)

    def step(j, carry, masked):
        off = pl.multiple_of(j * tq, tq)
        out = []
        for t in range(n_heads):
            m, acc = carry[t]
            s = lax.dot_general(q_h[t], kx_ref[t, pl.ds(off, tq), :], (((1,), (1,)), ((), ())),
                                preferred_element_type=F32)
            if masked:
                s = jnp.where(causal, s, NEG_INF)
            m_new = jnp.maximum(m, jnp.max(s, axis=1, keepdims=True) + fq_h[t])
            pexp = jnp.exp2(s + (fq_h[t] - m_new))
            acc = jnp.exp2(m - m_new) * acc + _dot(pexp.astype(BF16), vx_ref[t, pl.ds(off, tq), :])
            out.append((m_new, acc))
        return tuple(out)

    init = tuple((jnp.full((tq, 1), NEG_INF, F32), jnp.zeros((tq, LANES), F32)) for _ in range(n_heads))
    carry = lax.fori_loop(0, qi, lambda j, c: step(j, c, False), init)
    accs = [acc for _, acc in step(qi, carry, True)]
    for t in range(0, n_heads, 2):
        out_a = accs[t] / accs[t][:, spare[0]:spare[0] + 1]
        out_b = accs[t + 1] / accs[t + 1][:, spare[1]:spare[1] + 1]
        o_ref[:, pair_lanes(t)] = jnp.where(first, out_a, out_b).astype(o_ref.dtype)


def _fox_attn(qkv, f_cum, *, seq, d_attn, tq=512, heads=4):
    t = qkv.shape[0]
    bsz = t // seq
    width = heads * HEAD_DIM
    n_groups = d_attn // width
    nq = seq // tq
    kern = functools.partial(_fox_attn_kernel, tq=tq)
    return pl.pallas_call(
        kern,
        grid=(bsz, n_groups, nq),
        in_specs=[pl.BlockSpec((tq, width), lambda b, g, i: (b * nq + i, g)),
                  pl.BlockSpec((seq, width), lambda b, g, i: (b, n_groups + g)),
                  pl.BlockSpec((seq, width), lambda b, g, i: (b, 2 * n_groups + g)),
                  pl.BlockSpec((seq, LANES), lambda b, g, i: (b, 0))],
        out_specs=pl.BlockSpec((tq, width), lambda b, g, i: (b * nq + i, g)),
        out_shape=jax.ShapeDtypeStruct((t, d_attn), BF16),
        scratch_shapes=[pltpu.VMEM((heads, seq, LANES), BF16), pltpu.VMEM((heads, seq, LANES), BF16)],
        compiler_params=_params(("arbitrary", "arbitrary", "arbitrary"), 48),
        name="fox_attn",
    )(qkv, qkv, qkv, f_cum)


def _conv_kernel(cur_ref, halo_ref, w_ref, cb_ref, lg_ref, lb_ref, o_ref, ybuf, cbuf, *, ts):
    qi = pl.program_id(1)
    c = cur_ref.shape[1]
    halo = halo_ref[...].astype(F32)
    ybuf[0:CONV_HALO, :] = jnp.where(qi > 0, halo, 0.0)
    ybuf[CONV_HALO:CONV_HALO + ts, :] = cur_ref[...].astype(F32)

    def lane_chunk(ci, _):
        lo = pl.multiple_of(ci * LANES, LANES)
        for r in range(ts // CONV_ROWS):
            acc = jnp.broadcast_to(cb_ref[:, pl.ds(lo, LANES)], (CONV_ROWS, LANES))
            for j in range(CONV_K):
                start = r * CONV_ROWS + CONV_HALO - (CONV_K - 1) + j
                acc = acc + w_ref[j:j + 1, pl.ds(lo, LANES)] * ybuf[start:start + CONV_ROWS, pl.ds(lo, LANES)]
            cbuf[r * CONV_ROWS:(r + 1) * CONV_ROWS, pl.ds(lo, LANES)] = acc
        return 0

    lax.fori_loop(0, c // LANES, lane_chunk, 0)
    y = cbuf[...]
    mu = jnp.mean(y, axis=-1, keepdims=True)
    yc = y - mu
    z = yc * lax.rsqrt(jnp.mean(yc * yc, axis=-1, keepdims=True) + EPS) * lg_ref[...] + lb_ref[...]
    o_ref[...] = _silu(z).astype(o_ref.dtype)


def _conv_branch(y, conv_w, conv_b, ln_g, ln_b, *, seq, ts=256):
    t, c = y.shape
    bsz = t // seq
    ns = seq // ts
    w = jnp.pad(conv_w.astype(F32), ((0, CONV_HALO - CONV_K), (0, 0)))
    kern = functools.partial(_conv_kernel, ts=ts)
    row = lambda a: a.astype(F32).reshape(1, c)
    return pl.pallas_call(
        kern,
        grid=(bsz, ns),
        in_specs=[pl.BlockSpec((ts, c), lambda b, i: (b * ns + i, 0)),
                  pl.BlockSpec((CONV_HALO, c),
                               lambda b, i: (jnp.maximum((b * ns + i) * (ts // CONV_HALO) - 1, 0), 0)),
                  pl.BlockSpec((CONV_HALO, c), lambda b, i: (0, 0)),
                  pl.BlockSpec((1, c), lambda b, i: (0, 0)),
                  pl.BlockSpec((1, c), lambda b, i: (0, 0)),
                  pl.BlockSpec((1, c), lambda b, i: (0, 0))],
        out_specs=pl.BlockSpec((ts, c), lambda b, i: (b * ns + i, 0)),
        out_shape=jax.ShapeDtypeStruct((t, c), BF16),
        scratch_shapes=[pltpu.VMEM((CONV_HALO + ts, c), F32), pltpu.VMEM((ts, c), F32)],
        compiler_params=_params(("arbitrary", "arbitrary"), 32),
        name="conv_branch",
    )(y, y, w, row(conv_b), row(ln_g), row(ln_b))


def _mix_out_kernel(z_ref, o_ref, sga_ref, sgb_ref, co_ref, ao_ref, wo_ref, x_ref, mod_ref, out_ref, acc_ref):
    j = pl.program_id(1)
    y_a = _dot(z_ref[...], co_ref[...])
    y_b = _dot(o_ref[...], ao_ref[...])
    merged = (sga_ref[...].astype(F32) * y_a + sgb_ref[...].astype(F32) * y_b).astype(BF16)

    @pl.when(j == 0)
    def _():
        acc_ref[...] = jnp.zeros_like(acc_ref)

    acc_ref[...] += _dot(merged, wo_ref[...])

    @pl.when(j == pl.num_programs(1) - 1)
    def _():
        out_ref[...] = x_ref[...] + mod_ref[2:3, :] * acc_ref[...]


def _mix_out(z, o, gates, conv_out, attn_out, w_out, layer, x, mod, *, seq, tm=512, tn=1024):
    t, d = x.shape
    c = z.shape[1]
    nj = d // tn
    tiles_per_batch = seq // tm
    return pl.pallas_call(
        _mix_out_kernel,
        grid=(t // tm, nj),
        in_specs=[pl.BlockSpec((tm, c), lambda i, j: (i, 0)),
                  pl.BlockSpec((tm, c), lambda i, j: (i, 0)),
                  pl.BlockSpec((tm, tn), lambda i, j: (i, j)),
                  pl.BlockSpec((tm, tn), lambda i, j: (i, nj + j)),
                  pl.BlockSpec((None, c, tn), lambda i, j: (layer, 0, j)),
                  pl.BlockSpec((None, c, tn), lambda i, j: (layer, 0, j)),
                  pl.BlockSpec((None, tn, d), lambda i, j: (layer, j, 0)),
                  pl.BlockSpec((tm, d), lambda i, j: (i, 0)),
                  pl.BlockSpec((None, 6, d), lambda i, j: (i // tiles_per_batch, 0, 0))],
        out_specs=pl.BlockSpec((tm, d), lambda i, j: (i, 0)),
        out_shape=jax.ShapeDtypeStruct((t, d), F32),
        scratch_shapes=[pltpu.VMEM((tm, d), F32)],
        compiler_params=_params(("arbitrary", "arbitrary"), 48),
        name="mix_out",
    )(z, o, gates, gates, conv_out, attn_out, w_out, x, mod)


def _ffn_kernel(x_ref, mod_ref, g_ref, w1_ref, w3_ref, w2_ref, out_ref, h_ref, acc_ref):
    j = pl.program_id(1)

    @pl.when(j == 0)
    def _():
        h_ref[...] = _norm_mod(x_ref[...], g_ref[...], mod_ref[3:4, :], mod_ref[4:5, :]).astype(BF16)
        acc_ref[...] = jnp.zeros_like(acc_ref)

    h = h_ref[...]
    hidden = (_silu(_dot(h, w1_ref[...])) * _dot(h, w3_ref[...])).astype(BF16)
    acc_ref[...] += _dot(hidden, w2_ref[...])

    @pl.when(j == pl.num_programs(1) - 1)
    def _():
        out_ref[...] = x_ref[...] + mod_ref[5:6, :] * acc_ref[...]


def _ffn(x, mod, g, w1, w3, w2, layer, *, seq, tm=512, tf=512):
    t, d = x.shape
    f = w1.shape[2]
    tiles_per_batch = seq // tm
    return pl.pallas_call(
        _ffn_kernel,
        grid=(t // tm, f // tf),
        in_specs=[pl.BlockSpec((tm, d), lambda i, j: (i, 0)),
                  pl.BlockSpec((None, 6, d), lambda i, j: (i // tiles_per_batch, 0, 0)),
                  pl.BlockSpec((1, d), lambda i, j: (0, 0)),
                  pl.BlockSpec((None, d, tf), lambda i, j: (layer, 0, j)),
                  pl.BlockSpec((None, d, tf), lambda i, j: (layer, 0, j)),
                  pl.BlockSpec((None, tf, d), lambda i, j: (layer, j, 0))],
        out_specs=pl.BlockSpec((tm, d), lambda i, j: (i, 0)),
        out_shape=jax.ShapeDtypeStruct((t, d), F32),
        scratch_shapes=[pltpu.VMEM((tm, d), BF16), pltpu.VMEM((tm, d), F32)],
        compiler_params=_params(("arbitrary", "arbitrary"), 48),
        name="ffn",
    )(x, mod, g, w1, w3, w2)


def _router_kernel(x_ref, mod_ref, g_ref, rw_ref, rb_ref, h_ref, idx_ref, wt_ref):
    h = _norm_mod(x_ref[...], g_ref[...], mod_ref[3:4, :], mod_ref[4:5, :])
    h_ref[...] = h
    logits = jnp.dot(h, rw_ref[...], preferred_element_type=F32, precision=lax.Precision.HIGHEST) + rb_ref[...]
    lane = lax.broadcasted_iota(jnp.int32, logits.shape, 1).astype(F32)
    lg = jnp.where(lane < N_EXPERTS, logits, -jnp.inf)
    v1 = jnp.max(lg, axis=1, keepdims=True)
    i1 = jnp.min(jnp.where(lg == v1, lane, float(LANES)), axis=1, keepdims=True)
    lg2 = jnp.where(lane == i1, -jnp.inf, lg)
    v2 = jnp.max(lg2, axis=1, keepdims=True)
    i2 = jnp.min(jnp.where(lg2 == v2, lane, float(LANES)), axis=1, keepdims=True)
    e2 = jnp.exp(v2 - v1)
    w1 = 1.0 / (1.0 + e2)
    w2 = e2 / (1.0 + e2)
    idx_ref[...] = jnp.where(lane == 0.0, i1, jnp.where(lane == 1.0, i2, 0.0)).astype(jnp.int32)
    wt_ref[...] = jnp.where(lane == 0.0, w1, jnp.where(lane == 1.0, w2, 0.0))


def _router(x, mod, g, router_w, router_b, *, seq, tm=512):
    t, d = x.shape
    tiles_per_batch = seq // tm
    rw = jnp.pad(router_w.astype(F32), ((0, 0), (0, LANES - N_EXPERTS)))
    rb = jnp.pad(router_b.astype(F32), (0, LANES - N_EXPERTS)).reshape(1, LANES)
    return pl.pallas_call(
        _router_kernel,
        grid=(t // tm,),
        in_specs=[pl.BlockSpec((tm, d), lambda i: (i, 0)),
                  pl.BlockSpec((None, 6, d), lambda i: (i // tiles_per_batch, 0, 0)),
                  pl.BlockSpec((1, d), lambda i: (0, 0)),
                  pl.BlockSpec((d, LANES), lambda i: (0, 0)),
                  pl.BlockSpec((1, LANES), lambda i: (0, 0))],
        out_specs=[pl.BlockSpec((tm, d), lambda i: (i, 0)),
                   pl.BlockSpec((tm, LANES), lambda i: (i, 0)),
                   pl.BlockSpec((tm, LANES), lambda i: (i, 0))],
        out_shape=[jax.ShapeDtypeStruct((t, d), F32),
                   jax.ShapeDtypeStruct((t, LANES), jnp.int32),
                   jax.ShapeDtypeStruct((t, LANES), F32)],
        compiler_params=_params(("arbitrary",), 40),
        name="router",
    )(x, mod, g, rw, rb)


def _dispatch_tables(idx, wts, *, tm, xrows):
    t = idx.shape[0]
    n_rows = TOP_K * t + N_EXPERTS * tm
    n_tiles = n_rows // tm
    flat_e = idx.reshape(-1)
    onehot = (flat_e[:, None] == jnp.arange(N_EXPERTS, dtype=jnp.int32)[None, :]).astype(jnp.int32)
    rank = jnp.sum((jnp.cumsum(onehot, axis=0) - onehot) * onehot, axis=1)
    counts = jnp.sum(onehot, axis=0)
    padded = ((counts + tm - 1) // tm) * tm
    ends = jnp.cumsum(padded)
    starts = ends - padded
    pos = starts[flat_e] + rank
    row_src = jnp.full((n_rows,), -1, jnp.int32).at[pos].set(jnp.arange(TOP_K * t, dtype=jnp.int32))
    valid = row_src >= 0
    src = jnp.maximum(row_src, 0)
    row_token = src // TOP_K
    row_dst = jnp.where(valid, (src % TOP_K) * t + row_token, -1)
    row_gate = jnp.where(valid, wts.reshape(-1)[src], 0.0).reshape(n_rows, 1)
    n_active = (ends[-1] // tm).astype(jnp.int32)
    tile_start = jnp.minimum(jnp.arange(n_tiles, dtype=jnp.int32), n_active - 1) * tm
    tile_expert = jnp.minimum(jnp.sum((tile_start[:, None] >= ends[None, :]).astype(jnp.int32), axis=1),
                              N_EXPERTS - 1)
    tile_rows = jnp.sum(valid.reshape(n_tiles, tm).astype(jnp.int32), axis=1)
    tile_token = jnp.pad(row_token.reshape(n_tiles, tm), ((0, 1), (0, xrows - tm))).reshape(-1)
    spare = TOP_K * t + jnp.arange(xrows, dtype=jnp.int32)
    dst = jnp.pad(row_dst.reshape(n_tiles, tm), ((0, 0), (0, xrows - tm)), constant_values=-1)
    dst = jnp.where(dst >= 0, dst, spare[None, :])
    prev_dst = jnp.concatenate([spare[None, :], dst], axis=0).reshape(-1)
    return tile_expert, n_active.reshape(1), tile_token, prev_dst, tile_rows, row_gate


def _moe_kernel(te_ref, na_ref, tok_ref, dst_ref, rows_ref, h_hbm, gate_ref, w1_ref, w3_ref, w2_ref, ys_hbm,
                xbuf, xbf, acc_ref, obuf, sem_in, sem_out, *, tm, sub, per_step):
    i = pl.program_id(0)
    j = pl.program_id(1)
    last_j = pl.num_programs(1) - 1
    n_active = na_ref[0]
    active = i < n_active
    xrows = xbuf.shape[0]

    def gather_row(r, tok):
        return pltpu.make_async_copy(h_hbm.at[pl.ds(tok, 1)], xbuf.at[pl.ds(r, 1)], sem_in)

    def scatter_row(r, dst):
        return pltpu.make_async_copy(obuf.at[pl.ds(r, 1)], ys_hbm.at[pl.ds(dst, 1)], sem_out)

    def wait_gathered():
        pltpu.make_async_copy(h_hbm.at[pl.ds(0, xrows)], xbuf, sem_in).wait()

    @pl.when(active & (j == 0))
    def _():
        @pl.when(i == 0)
        def _():
            obuf[...] = jnp.zeros_like(obuf)

            def issue(r, _):
                gather_row(r, tok_ref[r]).start()
                return 0

            lax.fori_loop(0, xrows, issue, 0, unroll=8)

        wait_gathered()
        xbf[...] = xbuf[0:tm, :].astype(BF16)

    @pl.when(active)
    def _():
        w1 = w1_ref[...].astype(BF16)
        w3 = w3_ref[...].astype(BF16)
        w2 = w2_ref[...].astype(BF16)
        n_valid = rows_ref[i]
        for s in range(tm // sub):
            rows = slice(s * sub, (s + 1) * sub)

            def compute(rows=rows, s=s):
                @pl.when(j == 0)
                def _():
                    acc_ref[rows, :] = jnp.zeros((sub, acc_ref.shape[1]), F32)

                x = xbf[rows, :]
                hidden = (_silu(_dot(x, w1)) * _dot(x, w3)).astype(BF16)
                acc_ref[rows, :] += _dot(hidden, w2)

                if s == 0:
                    for u in range(per_step):
                        r = j * per_step + u
                        gather_row(r, tok_ref[(i + 1) * xrows + r]).start()
                        scatter_row(r, dst_ref[i * xrows + r]).start()

                @pl.when(j == last_j)
                def _():
                    if s == 0:
                        pltpu.make_async_copy(obuf, ys_hbm.at[pl.ds(0, xrows)], sem_out).wait()
                    obuf[rows, :] = acc_ref[rows, :] * gate_ref[rows, :]

            if s == 0:
                compute()
            else:
                pl.when(n_valid > s * sub)(compute)

    @pl.when(active & (j == last_j) & (i == n_active - 1))
    def _():
        n_valid = rows_ref[i]
        base = (i + 1) * xrows

        def issue(r, _):
            scatter_row(r, dst_ref[base + r]).start()
            return 0

        def wait(r, _):
            scatter_row(0, 0).wait()
            return 0

        lax.fori_loop(0, n_valid, issue, 0)
        wait_gathered()
        lax.fori_loop(0, n_valid, wait, 0)


def _moe_rows_per_step(tm, f, tf):
    nf = f // tf
    return -(-tm // (8 * nf)) * 8


def _moe_experts(h, tables, w1, w3, w2, layer, *, tm, sub=512, tf=256):
    t, d = h.shape
    f = w1.shape[3]
    nf = f // tf
    assert nf >= 2 and tm % sub == 0
    per_step = _moe_rows_per_step(tm, f, tf)
    xrows = nf * per_step
    tile_expert, n_active, tile_token, prev_dst, tile_rows, row_gate = tables
    n_tiles = tile_expert.shape[0]
    assert tile_token.shape[0] == (n_tiles + 1) * xrows
    kern = functools.partial(_moe_kernel, tm=tm, sub=sub, per_step=per_step)

    def f_idx(i, j, na):
        return jnp.where(i < na[0], j, nf - 1)

    grid_spec = pltpu.PrefetchScalarGridSpec(
        num_scalar_prefetch=5,
        grid=(n_tiles, nf),
        in_specs=[pl.BlockSpec(memory_space=pl.ANY),
                  pl.BlockSpec((tm, 1), lambda i, j, te, na, *_: (i, 0)),
                  pl.BlockSpec((None, None, d, tf), lambda i, j, te, na, *_: (layer, te[i], 0, f_idx(i, j, na))),
                  pl.BlockSpec((None, None, d, tf), lambda i, j, te, na, *_: (layer, te[i], 0, f_idx(i, j, na))),
                  pl.BlockSpec((None, None, tf, d), lambda i, j, te, na, *_: (layer, te[i], f_idx(i, j, na), 0))],
        out_specs=pl.BlockSpec(memory_space=pl.ANY),
        scratch_shapes=[pltpu.VMEM((xrows, d), F32), pltpu.VMEM((tm, d), BF16), pltpu.VMEM((tm, d), F32),
                        pltpu.VMEM((xrows, d), F32), pltpu.SemaphoreType.DMA(()), pltpu.SemaphoreType.DMA(())],
    )
    return pl.pallas_call(
        kern,
        grid_spec=grid_spec,
        out_shape=jax.ShapeDtypeStruct((TOP_K * t + xrows, d), F32),
        compiler_params=_params(("arbitrary", "arbitrary"), 58),
        name="moe_experts",
    )(tile_expert, n_active, tile_token, prev_dst, tile_rows, h, row_gate, w1, w3, w2)


def _combine_kernel(x_ref, y0_ref, y1_ref, mod_ref, g_ref, out_ref, *, final):
    x = x_ref[...] + mod_ref[5:6, :] * (y0_ref[...] + y1_ref[...])
    if final:
        x = x * lax.rsqrt(jnp.mean(x * x, axis=-1, keepdims=True) + EPS) * g_ref[...]
    out_ref[...] = x


def _combine(x, ys, mod, final_g, *, seq, final, tm=512):
    t, d = x.shape
    nt = t // tm
    tiles_per_batch = seq // tm
    return pl.pallas_call(
        functools.partial(_combine_kernel, final=final),
        grid=(nt,),
        in_specs=[pl.BlockSpec((tm, d), lambda i: (i, 0)),
                  pl.BlockSpec((tm, d), lambda i: (i, 0)),
                  pl.BlockSpec((tm, d), lambda i: (nt + i, 0)),
                  pl.BlockSpec((None, 6, d), lambda i: (i // tiles_per_batch, 0, 0)),
                  pl.BlockSpec((1, d), lambda i: (0, 0))],
        out_specs=pl.BlockSpec((tm, d), lambda i: (i, 0)),
        out_shape=jax.ShapeDtypeStruct((t, d), F32),
        compiler_params=_params(("arbitrary",), 48),
        name="moe_combine",
    )(x, ys, ys, mod, final_g)


def _final_norm_kernel(x_ref, g_ref, o_ref):
    x = x_ref[...]
    o_ref[...] = x * lax.rsqrt(jnp.mean(x * x, axis=-1, keepdims=True) + EPS) * g_ref[...]


def _final_norm(x, g, tm=512):
    t, d = x.shape
    return pl.pallas_call(
        _final_norm_kernel,
        grid=(t // tm,),
        in_specs=[pl.BlockSpec((tm, d), lambda i: (i, 0)), pl.BlockSpec((1, d), lambda i: (0, 0))],
        out_specs=pl.BlockSpec((tm, d), lambda i: (i, 0)),
        out_shape=jax.ShapeDtypeStruct((t, d), F32),
        compiler_params=_params(("arbitrary",), 32),
        name="final_norm",
    )(x, g)


def kernel(x, c, ada_w, ada_b, norm_mix_g, norm_ffn_g, w_in, b_f, conv_w, conv_b, conv_ln_g, conv_ln_b,
           conv_out, attn_out, w_out, ffn_w1, ffn_w3, ffn_w2, router_w, router_b, moe_w1, moe_w3, moe_w2,
           final_norm_g):
    bsz, seq, d = x.shape
    depth = ada_w.shape[0]
    d_conv = conv_w.shape[2]
    d_attn = attn_out.shape[1]
    t = bsz * seq
    moe_tm, moe_tf = 1024, 256
    final_g = final_norm_g.reshape(1, d)
    conv_out_b, attn_out_b, w_out_b = (w.astype(BF16) for w in (conv_out, attn_out, w_out))
    ffn_w1_b, ffn_w3_b, ffn_w2_b = (w.astype(BF16) for w in (ffn_w1, ffn_w3, ffn_w2))

    mods = _adaln(c, ada_w, ada_b).reshape(depth, bsz, 6, d)
    xf = x.reshape(t, d)
    for l in range(depth):
        mod = mods[l]
        w_parts = _split_w_in(w_in[l], d_attn=d_attn, d_conv=d_conv)
        qkv, glu, gates, fg = _mix_in(xf, mod, norm_mix_g[l].reshape(1, d), w_parts, seq=seq)
        z = _conv_branch(glu, conv_w[l], conv_b[l], conv_ln_g[l], conv_ln_b[l], seq=seq)
        o = _fox_attn(qkv, _fox_gate(fg, b_f[l], seq=seq), seq=seq, d_attn=d_attn)
        xf = _mix_out(z, o, gates, conv_out_b, attn_out_b, w_out_b, l, xf, mod, seq=seq)
        g_ffn = norm_ffn_g[l].reshape(1, d)
        i = l // 2
        if l % 2 == 0:
            xf = _ffn(xf, mod, g_ffn, ffn_w1_b, ffn_w3_b, ffn_w2_b, i, seq=seq)
        else:
            h, idx, wts = _router(xf, mod, g_ffn, router_w[i], router_b[i], seq=seq)
            xrows = (moe_w1.shape[3] // moe_tf) * _moe_rows_per_step(moe_tm, moe_w1.shape[3], moe_tf)
            tables = _dispatch_tables(idx[:, :TOP_K], wts[:, :TOP_K], tm=moe_tm, xrows=xrows)
            ys = _moe_experts(h, tables, moe_w1, moe_w3, moe_w2, i, tm=moe_tm, tf=moe_tf)
            xf = _combine(xf, ys, mod, final_g, seq=seq, final=(l == depth - 1))
    if depth % 2 == 1:
        xf = _final_norm(xf, final_g)
    return xf.reshape(bsz, seq, d)
```

```python
import functools

import jax
import jax.numpy as jnp
from jax import lax
from jax.experimental import pallas as pl
from jax.experimental.pallas import tpu as pltpu

EPS = 1e-6
NEG_INF = -1e30
LOG2E = 1.4426950408889634
N_HEADS = 16
HEAD_DIM = 64
CONV_K = 31
N_EXPERTS = 8
TOP_K = 2

LANES = 128
CONV_HALO = 32
CONV_ROWS = 64
MIB = 1024 * 1024

F32 = jnp.float32
BF16 = jnp.bfloat16


def _params(semantics, vmem_mib):
    return pltpu.CompilerParams(dimension_semantics=semantics, vmem_limit_bytes=vmem_mib * MIB)


def _dot(a, b):
    return jnp.dot(a, b, preferred_element_type=F32)


def _silu(x):
    return x * jax.nn.sigmoid(x)


def _norm_mod(x, g, shift, scale):
    y = x * lax.rsqrt(jnp.mean(x * x, axis=-1, keepdims=True) + EPS) * g
    return y * (1.0 + scale) + shift


def _adaln_kernel(ct_ref, w_ref, b_ref, o_ref):
    @pl.when(pl.program_id(1) == 0)
    def _():
        o_ref[...] = jnp.broadcast_to(b_ref[...], o_ref.shape)

    w = w_ref[...]
    cols = _silu(ct_ref[...])
    for b in range(o_ref.shape[0]):
        o_ref[b:b + 1, :] += jnp.sum(w * cols[:, b:b + 1], axis=0, keepdims=True)


def _adaln(c, ada_w, ada_b, tk=128):
    depth, d, n = ada_w.shape
    bsz = c.shape[0]
    return pl.pallas_call(
        _adaln_kernel,
        grid=(depth, d // tk),
        in_specs=[pl.BlockSpec((tk, bsz), lambda l, k: (k, 0)),
                  pl.BlockSpec((None, tk, n), lambda l, k: (l, k, 0)),
                  pl.BlockSpec((None, 1, n), lambda l, k: (l, 0, 0))],
        out_specs=pl.BlockSpec((None, bsz, n), lambda l, k: (l, 0, 0)),
        out_shape=jax.ShapeDtypeStruct((depth, bsz, n), F32),
        compiler_params=_params(("arbitrary", "arbitrary"), 40),
        name="adaln",
    )(c.T, ada_w, ada_b.reshape(depth, 1, n))


def _mix_in_kernel(x_ref, mod_ref, g_ref, wqkv_ref, wa_ref, wg_ref, wgate_ref, wfg_ref,
                   qkv_ref, glu_ref, gate_ref, fg_ref, h_ref, *, n_q, n_qkv, n_glu):
    j = pl.program_id(1)

    @pl.when(j == 0)
    def _():
        h = _norm_mod(x_ref[...], g_ref[...], mod_ref[0:1, :], mod_ref[1:2, :]).astype(BF16)
        h_ref[...] = h
        fg_ref[...] = _dot(h, wfg_ref[...])

    @pl.when(j < n_q)
    def _():
        qkv_ref[...] = (_dot(h_ref[...], wqkv_ref[...]) * (HEAD_DIM ** -0.5 * LOG2E)).astype(BF16)

    @pl.when((j >= n_q) & (j < n_qkv))
    def _():
        qkv_ref[...] = _dot(h_ref[...], wqkv_ref[...]).astype(BF16)

    @pl.when((j >= n_qkv) & (j < n_qkv + n_glu))
    def _():
        h = h_ref[...]
        glu_ref[...] = (_dot(h, wa_ref[...]) * jax.nn.sigmoid(_dot(h, wg_ref[...]))).astype(BF16)

    @pl.when(j >= n_qkv + n_glu)
    def _():
        gate_ref[...] = jax.nn.sigmoid(_dot(h_ref[...], wgate_ref[...])).astype(BF16)


def _mix_in(x, mod, g, w_parts, *, seq, tm=1024):
    t, d = x.shape
    w_qkv, w_a, w_g, w_gate, w_fg = w_parts
    n_qkv, _, tn = w_qkv.shape
    n_glu, n_gate = w_a.shape[0], w_gate.shape[0]
    n_q, d_attn, d_conv = n_qkv // 3, n_qkv * tn // 3, n_glu * tn // 2
    tiles_per_batch = seq // tm
    glu_idx = lambda j: jnp.clip(j - n_qkv, 0, n_glu - 1)
    gate_idx = lambda j: jnp.clip(j - n_qkv - n_glu, 0, n_gate - 1)
    kern = functools.partial(_mix_in_kernel, n_q=n_q, n_qkv=n_qkv, n_glu=n_glu)
    return pl.pallas_call(
        kern,
        grid=(t // tm, n_qkv + n_glu + n_gate),
        in_specs=[pl.BlockSpec((tm, d), lambda i, j: (i, 0)),
                  pl.BlockSpec((None, 6, d), lambda i, j: (i // tiles_per_batch, 0, 0)),
                  pl.BlockSpec((1, d), lambda i, j: (0, 0)),
                  pl.BlockSpec((None, d, tn), lambda i, j: (jnp.minimum(j, n_qkv - 1), 0, 0)),
                  pl.BlockSpec((None, d, tn // 2), lambda i, j: (glu_idx(j), 0, 0)),
                  pl.BlockSpec((None, d, tn // 2), lambda i, j: (glu_idx(j), 0, 0)),
                  pl.BlockSpec((None, d, tn), lambda i, j: (gate_idx(j), 0, 0)),
                  pl.BlockSpec((d, LANES), lambda i, j: (0, 0))],
        out_specs=[pl.BlockSpec((tm, tn), lambda i, j: (i, jnp.minimum(j, n_qkv - 1))),
                   pl.BlockSpec((tm, tn // 2), lambda i, j: (i, glu_idx(j))),
                   pl.BlockSpec((tm, tn), lambda i, j: (i, gate_idx(j))),
                   pl.BlockSpec((tm, LANES), lambda i, j: (i, 0))],
        out_shape=[jax.ShapeDtypeStruct((t, 3 * d_attn), BF16),
                   jax.ShapeDtypeStruct((t, d_conv), BF16),
                   jax.ShapeDtypeStruct((t, 2 * d), BF16),
                   jax.ShapeDtypeStruct((t, LANES), F32)],
        scratch_shapes=[pltpu.VMEM((tm, d), BF16)],
        compiler_params=_params(("arbitrary", "arbitrary"), 48),
        name="mix_in",
    )(x, mod, g, w_qkv, w_a, w_g, w_gate, w_fg)


def _tile_major(w, tn):
    d, n = w.shape
    return w.astype(BF16).reshape(d, n // tn, tn).transpose(1, 0, 2)


def _split_w_in(w, *, d_attn, d_conv, tn=512):
    o = 3 * d_attn
    w_fg = jnp.pad(w[:, o:o + N_HEADS].astype(BF16), ((0, 0), (0, LANES - N_HEADS)))
    u = o + N_HEADS
    g = u + 2 * d_conv
    return (_tile_major(w[:, :o], tn), _tile_major(w[:, u:u + d_conv], tn // 2),
            _tile_major(w[:, u + d_conv:g], tn // 2), _tile_major(w[:, g:], tn), w_fg)


def _fox_gate_kernel(fg_ref, bf_ref, o_ref, carry_ref):
    @pl.when(pl.program_id(1) == 0)
    def _():
        carry_ref[...] = jnp.zeros_like(carry_ref)

    x = fg_ref[...] + bf_ref[...]
    logf = jnp.minimum(x, 0.0) - jnp.log(1.0 + jnp.exp(-jnp.abs(x)))
    n = x.shape[0]
    tri = (lax.broadcasted_iota(jnp.int32, (n, n), 1) <= lax.broadcasted_iota(jnp.int32, (n, n), 0)).astype(BF16)
    hi = logf.astype(BF16)
    r1 = logf - hi.astype(F32)
    mid = r1.astype(BF16)
    lo = (r1 - mid.astype(F32)).astype(BF16)
    cs = _dot(tri, hi) + _dot(tri, mid) + _dot(tri, lo) + carry_ref[...]
    o_ref[...] = cs
    carry_ref[...] = cs[n - 1:n, :]


def _fox_gate(fg, b_f, *, seq, chunk=256):
    t = fg.shape[0]
    bsz = t // seq
    bf = jnp.pad(b_f.astype(F32), (0, LANES - b_f.shape[0])).reshape(1, LANES)
    return pl.pallas_call(
        _fox_gate_kernel,
        grid=(bsz, seq // chunk),
        in_specs=[pl.BlockSpec((chunk, LANES), lambda b, i: (b * (seq // chunk) + i, 0)),
                  pl.BlockSpec((1, LANES), lambda b, i: (0, 0))],
        out_specs=pl.BlockSpec((chunk, LANES), lambda b, i: (b * (seq // chunk) + i, 0)),
        out_shape=jax.ShapeDtypeStruct((t, LANES), F32),
        scratch_shapes=[pltpu.VMEM((1, LANES), F32)],
        compiler_params=_params(("arbitrary", "arbitrary"), 16),
        name="fox_gate",
    )(fg, bf)


def _split3(x):
    hi = x.astype(BF16)
    r = x - hi.astype(F32)
    mid = r.astype(BF16)
    return hi, mid, (r - mid.astype(F32)).astype(BF16)


def _fox_attn_kernel(q_ref, k_ref, v_ref, f_ref, o_ref, kx_ref, vx_ref, *, tq):
    g = pl.program_id(1)
    qi = pl.program_id(2)
    n_heads = kx_ref.shape[0]
    lane = lax.broadcasted_iota(jnp.int32, (1, LANES), 1)
    first = lane < HEAD_DIM
    data = (first, jnp.logical_not(first))
    spare = (HEAD_DIM, 0)

    def pair_lanes(t):
        return slice((t // 2) * LANES, (t // 2 + 1) * LANES)

    @pl.when(qi == 0)
    def _():
        f_all = f_ref[...]
        for t in range(n_heads):
            k = k_ref[:, pair_lanes(t)].astype(F32)
            v = v_ref[:, pair_lanes(t)].astype(F32)
            f_key = jnp.sum(jnp.where(lane == n_heads * g + t, f_all, 0.0), axis=1, keepdims=True) * (-LOG2E)
            hi, mid, lo = (piece.astype(F32) for piece in _split3(f_key))
            s0 = spare[t % 2]
            bias = jnp.where(lane == s0, hi, jnp.where(lane == s0 + 1, mid, jnp.where(lane == s0 + 2, lo, 0.0)))
            kx_ref[t] = jnp.where(data[t % 2], k, bias).astype(BF16)
            vx_ref[t] = jnp.where(data[t % 2], v, jnp.where(lane == s0, 1.0, 0.0)).astype(BF16)

    f_rows = f_ref[pl.ds(pl.multiple_of(qi * tq, tq), tq), :]
    q_h, fq_h = [], []
    for t in range(n_heads):
        q = q_ref[:, pair_lanes(t)].astype(F32)
        s0 = spare[t % 2]
        ones = (lane >= s0) & (lane < s0 + 3)
        q_h.append(jnp.where(data[t % 2], q, jnp.where(ones, 1.0, 0.0)).astype(BF16))
        fq_h.append(jnp.sum(jnp.where(lane == n_heads * g + t, f_rows, 0.0), axis=1, keepdims=True) * LOG2E)
    causal = lax.broadcasted_iota(jnp.int32, (tq, tq), 1) <= lax.broadcasted_iota(jnp.int32, (tq, tq), 0)

    def step(j, carry, masked):
        off = pl.multiple_of(j * tq, tq)
        out = []
        for t in range(n_heads):
            m, acc = carry[t]
            s = lax.dot_general(q_h[t], kx_ref[t, pl.ds(off, tq), :], (((1,), (1,)), ((), ())),
                                preferred_element_type=F32)
            if masked:
                s = jnp.where(causal, s, NEG_INF)
            m_new = jnp.maximum(m, jnp.max(s, axis=1, keepdims=True) + fq_h[t])
            pexp = jnp.exp2(s + (fq_h[t] - m_new))
            acc = jnp.exp2(m - m_new) * acc + _dot(pexp.astype(BF16), vx_ref[t, pl.ds(off, tq), :])
            out.append((m_new, acc))
        return tuple(out)

    init = tuple((jnp.full((tq, 1), NEG_INF, F32), jnp.zeros((tq, LANES), F32)) for _ in range(n_heads))
    carry = lax.fori_loop(0, qi, lambda j, c: step(j, c, False), init)
    accs = [acc for _, acc in step(qi, carry, True)]
    for t in range(0, n_heads, 2):
        out_a = accs[t] / accs[t][:, spare[0]:spare[0] + 1]
        out_b = accs[t + 1] / accs[t + 1][:, spare[1]:spare[1] + 1]
        o_ref[:, pair_lanes(t)] = jnp.where(first, out_a, out_b).astype(o_ref.dtype)


def _fox_attn(qkv, f_cum, *, seq, d_attn, tq=512, heads=4):
    t = qkv.shape[0]
    bsz = t // seq
    width = heads * HEAD_DIM
    n_groups = d_attn // width
    nq = seq // tq
    kern = functools.partial(_fox_attn_kernel, tq=tq)
    return pl.pallas_call(
        kern,
        grid=(bsz, n_groups, nq),
        in_specs=[pl.BlockSpec((tq, width), lambda b, g, i: (b * nq + i, g)),
                  pl.BlockSpec((seq, width), lambda b, g, i: (b, n_groups + g)),
                  pl.BlockSpec((seq, width), lambda b, g, i: (b, 2 * n_groups + g)),
                  pl.BlockSpec((seq, LANES), lambda b, g, i: (b, 0))],
        out_specs=pl.BlockSpec((tq, width), lambda b, g, i: (b * nq + i, g)),
        out_shape=jax.ShapeDtypeStruct((t, d_attn), BF16),
        scratch_shapes=[pltpu.VMEM((heads, seq, LANES), BF16), pltpu.VMEM((heads, seq, LANES), BF16)],
        compiler_params=_params(("arbitrary", "arbitrary", "arbitrary"), 48),
        name="fox_attn",
    )(qkv, qkv, qkv, f_cum)


def _conv_kernel(cur_ref, halo_ref, w_ref, cb_ref, lg_ref, lb_ref, o_ref, ybuf, cbuf, *, ts):
    qi = pl.program_id(1)
    c = cur_ref.shape[1]
    halo = halo_ref[...].astype(F32)
    ybuf[0:CONV_HALO, :] = jnp.where(qi > 0, halo, 0.0)
    ybuf[CONV_HALO:CONV_HALO + ts, :] = cur_ref[...].astype(F32)

    def lane_chunk(ci, _):
        lo = pl.multiple_of(ci * LANES, LANES)
        for r in range(ts // CONV_ROWS):
            acc = jnp.broadcast_to(cb_ref[:, pl.ds(lo, LANES)], (CONV_ROWS, LANES))
            for j in range(CONV_K):
                start = r * CONV_ROWS + CONV_HALO - (CONV_K - 1) + j
                acc = acc + w_ref[j:j + 1, pl.ds(lo, LANES)] * ybuf[start:start + CONV_ROWS, pl.ds(lo, LANES)]
            cbuf[r * CONV_ROWS:(r + 1) * CONV_ROWS, pl.ds(lo, LANES)] = acc
        return 0

    lax.fori_loop(0, c // LANES, lane_chunk, 0)
    y = cbuf[...]
    mu = jnp.mean(y, axis=-1, keepdims=True)
    yc = y - mu
    z = yc * lax.rsqrt(jnp.mean(yc * yc, axis=-1, keepdims=True) + EPS) * lg_ref[...] + lb_ref[...]
    o_ref[...] = _silu(z).astype(o_ref.dtype)


def _conv_branch(y, conv_w, conv_b, ln_g, ln_b, *, seq, ts=256):
    t, c = y.shape
    bsz = t // seq
    ns = seq // ts
    w = jnp.pad(conv_w.astype(F32), ((0, CONV_HALO - CONV_K), (0, 0)))
    kern = functools.partial(_conv_kernel, ts=ts)
    row = lambda a: a.astype(F32).reshape(1, c)
    return pl.pallas_call(
        kern,
        grid=(bsz, ns),
        in_specs=[pl.BlockSpec((ts, c), lambda b, i: (b * ns + i, 0)),
                  pl.BlockSpec((CONV_HALO, c),
                               lambda b, i: (jnp.maximum((b * ns + i) * (ts // CONV_HALO) - 1, 0), 0)),
                  pl.BlockSpec((CONV_HALO, c), lambda b, i: (0, 0)),
                  pl.BlockSpec((1, c), lambda b, i: (0, 0)),
                  pl.BlockSpec((1, c), lambda b, i: (0, 0)),
                  pl.BlockSpec((1, c), lambda b, i: (0, 0))],
        out_specs=pl.BlockSpec((ts, c), lambda b, i: (b * ns + i, 0)),
        out_shape=jax.ShapeDtypeStruct((t, c), BF16),
        scratch_shapes=[pltpu.VMEM((CONV_HALO + ts, c), F32), pltpu.VMEM((ts, c), F32)],
        compiler_params=_params(("arbitrary", "arbitrary"), 32),
        name="conv_branch",
    )(y, y, w, row(conv_b), row(ln_g), row(ln_b))


def _mix_out_kernel(z_ref, o_ref, sga_ref, sgb_ref, co_ref, ao_ref, wo_ref, x_ref, mod_ref, out_ref, acc_ref):
    j = pl.program_id(1)
    y_a = _dot(z_ref[...], co_ref[...])
    y_b = _dot(o_ref[...], ao_ref[...])
    merged = (sga_ref[...].astype(F32) * y_a + sgb_ref[...].astype(F32) * y_b).astype(BF16)

    @pl.when(j == 0)
    def _():
        acc_ref[...] = jnp.zeros_like(acc_ref)

    acc_ref[...] += _dot(merged, wo_ref[...])

    @pl.when(j == pl.num_programs(1) - 1)
    def _():
        out_ref[...] = x_ref[...] + mod_ref[2:3, :] * acc_ref[...]


def _mix_out(z, o, gates, conv_out, attn_out, w_out, layer, x, mod, *, seq, tm=512, tn=1024):
    t, d = x.shape
    c = z.shape[1]
    nj = d // tn
    tiles_per_batch = seq // tm
    return pl.pallas_call(
        _mix_out_kernel,
        grid=(t // tm, nj),
        in_specs=[pl.BlockSpec((tm, c), lambda i, j: (i, 0)),
                  pl.BlockSpec((tm, c), lambda i, j: (i, 0)),
                  pl.BlockSpec((tm, tn), lambda i, j: (i, j)),
                  pl.BlockSpec((tm, tn), lambda i, j: (i, nj + j)),
                  pl.BlockSpec((None, c, tn), lambda i, j: (layer, 0, j)),
                  pl.BlockSpec((None, c, tn), lambda i, j: (layer, 0, j)),
                  pl.BlockSpec((None, tn, d), lambda i, j: (layer, j, 0)),
                  pl.BlockSpec((tm, d), lambda i, j: (i, 0)),
                  pl.BlockSpec((None, 6, d), lambda i, j: (i // tiles_per_batch, 0, 0))],
        out_specs=pl.BlockSpec((tm, d), lambda i, j: (i, 0)),
        out_shape=jax.ShapeDtypeStruct((t, d), F32),
        scratch_shapes=[pltpu.VMEM((tm, d), F32)],
        compiler_params=_params(("arbitrary", "arbitrary"), 48),
        name="mix_out",
    )(z, o, gates, gates, conv_out, attn_out, w_out, x, mod)


def _ffn_kernel(x_ref, mod_ref, g_ref, w1_ref, w3_ref, w2_ref, out_ref, h_ref, acc_ref):
    j = pl.program_id(1)

    @pl.when(j == 0)
    def _():
        h_ref[...] = _norm_mod(x_ref[...], g_ref[...], mod_ref[3:4, :], mod_ref[4:5, :]).astype(BF16)
        acc_ref[...] = jnp.zeros_like(acc_ref)

    h = h_ref[...]
    hidden = (_silu(_dot(h, w1_ref[...])) * _dot(h, w3_ref[...])).astype(BF16)
    acc_ref[...] += _dot(hidden, w2_ref[...])

    @pl.when(j == pl.num_programs(1) - 1)
    def _():
        out_ref[...] = x_ref[...] + mod_ref[5:6, :] * acc_ref[...]


def _ffn(x, mod, g, w1, w3, w2, layer, *, seq, tm=512):
    t, d = x.shape
    tf = w1.shape[3]
    f = w1.shape[1] * tf
    tiles_per_batch = seq // tm
    return pl.pallas_call(
        _ffn_kernel,
        grid=(t // tm, f // tf),
        in_specs=[pl.BlockSpec((tm, d), lambda i, j: (i, 0)),
                  pl.BlockSpec((None, 6, d), lambda i, j: (i // tiles_per_batch, 0, 0)),
                  pl.BlockSpec((1, d), lambda i, j: (0, 0)),
                  pl.BlockSpec((None, None, d, tf), lambda i, j: (layer, j, 0, 0)),
                  pl.BlockSpec((None, None, d, tf), lambda i, j: (layer, j, 0, 0)),
                  pl.BlockSpec((None, tf, d), lambda i, j: (layer, j, 0))],
        out_specs=pl.BlockSpec((tm, d), lambda i, j: (i, 0)),
        out_shape=jax.ShapeDtypeStruct((t, d), F32),
        scratch_shapes=[pltpu.VMEM((tm, d), BF16), pltpu.VMEM((tm, d), F32)],
        compiler_params=_params(("arbitrary", "arbitrary"), 48),
        name="ffn",
    )(x, mod, g, w1, w3, w2)


def _router_kernel(x_ref, mod_ref, g_ref, rw_ref, rb_ref, h_ref, idx_ref, wt_ref):
    h = _norm_mod(x_ref[...], g_ref[...], mod_ref[3:4, :], mod_ref[4:5, :])
    h_ref[...] = h
    logits = jnp.dot(h, rw_ref[...], preferred_element_type=F32, precision=lax.Precision.HIGHEST) + rb_ref[...]
    lane = lax.broadcasted_iota(jnp.int32, logits.shape, 1).astype(F32)
    lg = jnp.where(lane < N_EXPERTS, logits, -jnp.inf)
    v1 = jnp.max(lg, axis=1, keepdims=True)
    i1 = jnp.min(jnp.where(lg == v1, lane, float(LANES)), axis=1, keepdims=True)
    lg2 = jnp.where(lane == i1, -jnp.inf, lg)
    v2 = jnp.max(lg2, axis=1, keepdims=True)
    i2 = jnp.min(jnp.where(lg2 == v2, lane, float(LANES)), axis=1, keepdims=True)
    e2 = jnp.exp(v2 - v1)
    w1 = 1.0 / (1.0 + e2)
    w2 = e2 / (1.0 + e2)
    idx_ref[...] = jnp.where(lane == 0.0, i1, jnp.where(lane == 1.0, i2, 0.0)).astype(jnp.int32)
    wt_ref[...] = jnp.where(lane == 0.0, w1, jnp.where(lane == 1.0, w2, 0.0))


def _router(x, mod, g, router_w, router_b, *, seq, tm=512):
    t, d = x.shape
    tiles_per_batch = seq // tm
    rw = jnp.pad(router_w.astype(F32), ((0, 0), (0, LANES - N_EXPERTS)))
    rb = jnp.pad(router_b.astype(F32), (0, LANES - N_EXPERTS)).reshape(1, LANES)
    return pl.pallas_call(
        _router_kernel,
        grid=(t // tm,),
        in_specs=[pl.BlockSpec((tm, d), lambda i: (i, 0)),
                  pl.BlockSpec((None, 6, d), lambda i: (i // tiles_per_batch, 0, 0)),
                  pl.BlockSpec((1, d), lambda i: (0, 0)),
                  pl.BlockSpec((d, LANES), lambda i: (0, 0)),
                  pl.BlockSpec((1, LANES), lambda i: (0, 0))],
        out_specs=[pl.BlockSpec((tm, d), lambda i: (i, 0)),
                   pl.BlockSpec((tm, LANES), lambda i: (i, 0)),
                   pl.BlockSpec((tm, LANES), lambda i: (i, 0))],
        out_shape=[jax.ShapeDtypeStruct((t, d), F32),
                   jax.ShapeDtypeStruct((t, LANES), jnp.int32),
                   jax.ShapeDtypeStruct((t, LANES), F32)],
        compiler_params=_params(("arbitrary",), 40),
        name="router",
    )(x, mod, g, rw, rb)


def _dispatch_tables(idx, wts, *, tm, xrows):
    t = idx.shape[0]
    n_rows = TOP_K * t + N_EXPERTS * tm
    n_tiles = n_rows // tm
    flat_e = idx.reshape(-1)
    onehot = (flat_e[:, None] == jnp.arange(N_EXPERTS, dtype=jnp.int32)[None, :]).astype(jnp.int32)
    rank = jnp.sum((jnp.cumsum(onehot, axis=0) - onehot) * onehot, axis=1)
    counts = jnp.sum(onehot, axis=0)
    padded = ((counts + tm - 1) // tm) * tm
    ends = jnp.cumsum(padded)
    starts = ends - padded
    pos = starts[flat_e] + rank
    row_src = jnp.full((n_rows,), -1, jnp.int32).at[pos].set(jnp.arange(TOP_K * t, dtype=jnp.int32))
    valid = row_src >= 0
    src = jnp.maximum(row_src, 0)
    row_token = src // TOP_K
    row_dst = jnp.where(valid, (src % TOP_K) * t + row_token, -1)
    row_gate = jnp.where(valid, wts.reshape(-1)[src], 0.0).reshape(n_rows, 1)
    n_active = (ends[-1] // tm).astype(jnp.int32)
    tile_start = jnp.minimum(jnp.arange(n_tiles, dtype=jnp.int32), n_active - 1) * tm
    tile_expert = jnp.minimum(jnp.sum((tile_start[:, None] >= ends[None, :]).astype(jnp.int32), axis=1),
                              N_EXPERTS - 1)
    tile_rows = jnp.sum(valid.reshape(n_tiles, tm).astype(jnp.int32), axis=1)
    tile_token = jnp.pad(row_token.reshape(n_tiles, tm), ((0, 1), (0, xrows - tm))).reshape(-1)
    spare = TOP_K * t + jnp.arange(xrows, dtype=jnp.int32)
    dst = jnp.pad(row_dst.reshape(n_tiles, tm), ((0, 0), (0, xrows - tm)), constant_values=-1)
    dst = jnp.where(dst >= 0, dst, spare[None, :])
    prev_dst = jnp.concatenate([spare[None, :], dst], axis=0).reshape(-1)
    return tile_expert, n_active.reshape(1), tile_token, prev_dst, tile_rows, row_gate


def _moe_kernel(te_ref, na_ref, tok_ref, dst_ref, rows_ref, h_hbm, gate_ref, w1_ref, w3_ref, w2_ref, ys_hbm,
                xbuf, xbf, acc_ref, obuf, sem_in, sem_out, *, tm, sub, per_step):
    i = pl.program_id(0)
    j = pl.program_id(1)
    last_j = pl.num_programs(1) - 1
    n_active = na_ref[0]
    active = i < n_active
    xrows = xbuf.shape[0]

    def gather_row(r, tok):
        return pltpu.make_async_copy(h_hbm.at[pl.ds(tok, 1)], xbuf.at[pl.ds(r, 1)], sem_in)

    def scatter_row(r, dst):
        return pltpu.make_async_copy(obuf.at[pl.ds(r, 1)], ys_hbm.at[pl.ds(dst, 1)], sem_out)

    def wait_gathered():
        pltpu.make_async_copy(h_hbm.at[pl.ds(0, xrows)], xbuf, sem_in).wait()

    @pl.when(active & (j == 0))
    def _():
        @pl.when(i == 0)
        def _():
            obuf[...] = jnp.zeros_like(obuf)

            def issue(r, _):
                gather_row(r, tok_ref[r]).start()
                return 0

            lax.fori_loop(0, xrows, issue, 0, unroll=8)

        wait_gathered()
        xbf[...] = xbuf[0:tm, :].astype(BF16)

    @pl.when(active)
    def _():
        w1 = w1_ref[...].astype(BF16)
        w3 = w3_ref[...].astype(BF16)
        w2 = w2_ref[...].astype(BF16)
        n_valid = rows_ref[i]
        for s in range(tm // sub):
            rows = slice(s * sub, (s + 1) * sub)

            def compute(rows=rows, s=s):
                @pl.when(j == 0)
                def _():
                    acc_ref[rows, :] = jnp.zeros((sub, acc_ref.shape[1]), F32)

                x = xbf[rows, :]
                hidden = (_silu(_dot(x, w1)) * _dot(x, w3)).astype(BF16)
                acc_ref[rows, :] += _dot(hidden, w2)

                if s == 0:
                    for u in range(per_step):
                        r = j * per_step + u
                        gather_row(r, tok_ref[(i + 1) * xrows + r]).start()
                        scatter_row(r, dst_ref[i * xrows + r]).start()

                @pl.when(j == last_j)
                def _():
                    if s == 0:
                        pltpu.make_async_copy(obuf, ys_hbm.at[pl.ds(0, xrows)], sem_out).wait()
                    obuf[rows, :] = acc_ref[rows, :] * gate_ref[rows, :]

            if s == 0:
                compute()
            else:
                pl.when(n_valid > s * sub)(compute)

    @pl.when(active & (j == last_j) & (i == n_active - 1))
    def _():
        n_valid = rows_ref[i]
        base = (i + 1) * xrows

        def issue(r, _):
            scatter_row(r, dst_ref[base + r]).start()
            return 0

        def wait(r, _):
            scatter_row(0, 0).wait()
            return 0

        lax.fori_loop(0, n_valid, issue, 0)
        wait_gathered()
        lax.fori_loop(0, n_valid, wait, 0)


def _moe_rows_per_step(tm, f, tf):
    nf = f // tf
    return -(-tm // (8 * nf)) * 8


def _moe_experts(h, tables, w1, w3, w2, layer, *, tm, sub=512, tf=256):
    t, d = h.shape
    f = w1.shape[3]
    nf = f // tf
    assert nf >= 2 and tm % sub == 0
    per_step = _moe_rows_per_step(tm, f, tf)
    xrows = nf * per_step
    tile_expert, n_active, tile_token, prev_dst, tile_rows, row_gate = tables
    n_tiles = tile_expert.shape[0]
    assert tile_token.shape[0] == (n_tiles + 1) * xrows
    kern = functools.partial(_moe_kernel, tm=tm, sub=sub, per_step=per_step)

    def f_idx(i, j, na):
        return jnp.where(i < na[0], j, nf - 1)

    grid_spec = pltpu.PrefetchScalarGridSpec(
        num_scalar_prefetch=5,
        grid=(n_tiles, nf),
        in_specs=[pl.BlockSpec(memory_space=pl.ANY),
                  pl.BlockSpec((tm, 1), lambda i, j, te, na, *_: (i, 0)),
                  pl.BlockSpec((None, None, d, tf), lambda i, j, te, na, *_: (layer, te[i], 0, f_idx(i, j, na))),
                  pl.BlockSpec((None, None, d, tf), lambda i, j, te, na, *_: (layer, te[i], 0, f_idx(i, j, na))),
                  pl.BlockSpec((None, None, tf, d), lambda i, j, te, na, *_: (layer, te[i], f_idx(i, j, na), 0))],
        out_specs=pl.BlockSpec(memory_space=pl.ANY),
        scratch_shapes=[pltpu.VMEM((xrows, d), F32), pltpu.VMEM((tm, d), BF16), pltpu.VMEM((tm, d), F32),
                        pltpu.VMEM((xrows, d), F32), pltpu.SemaphoreType.DMA(()), pltpu.SemaphoreType.DMA(())],
    )
    return pl.pallas_call(
        kern,
        grid_spec=grid_spec,
        out_shape=jax.ShapeDtypeStruct((TOP_K * t + xrows, d), F32),
        compiler_params=_params(("arbitrary", "arbitrary"), 58),
        name="moe_experts",
    )(tile_expert, n_active, tile_token, prev_dst, tile_rows, h, row_gate, w1, w3, w2)


def _combine_kernel(x_ref, y0_ref, y1_ref, mod_ref, g_ref, out_ref, *, final):
    x = x_ref[...] + mod_ref[5:6, :] * (y0_ref[...] + y1_ref[...])
    if final:
        x = x * lax.rsqrt(jnp.mean(x * x, axis=-1, keepdims=True) + EPS) * g_ref[...]
    out_ref[...] = x


def _combine(x, ys, mod, final_g, *, seq, final, tm=512):
    t, d = x.shape
    nt = t // tm
    tiles_per_batch = seq // tm
    return pl.pallas_call(
        functools.partial(_combine_kernel, final=final),
        grid=(nt,),
        in_specs=[pl.BlockSpec((tm, d), lambda i: (i, 0)),
                  pl.BlockSpec((tm, d), lambda i: (i, 0)),
                  pl.BlockSpec((tm, d), lambda i: (nt + i, 0)),
                  pl.BlockSpec((None, 6, d), lambda i: (i // tiles_per_batch, 0, 0)),
                  pl.BlockSpec((1, d), lambda i: (0, 0))],
        out_specs=pl.BlockSpec((tm, d), lambda i: (i, 0)),
        out_shape=jax.ShapeDtypeStruct((t, d), F32),
        compiler_params=_params(("arbitrary",), 48),
        name="moe_combine",
    )(x, ys, ys, mod, final_g)


def _final_norm_kernel(x_ref, g_ref, o_ref):
    x = x_ref[...]
    o_ref[...] = x * lax.rsqrt(jnp.mean(x * x, axis=-1, keepdims=True) + EPS) * g_ref[...]


def _final_norm(x, g, tm=512):
    t, d = x.shape
    return pl.pallas_call(
        _final_norm_kernel,
        grid=(t // tm,),
        in_specs=[pl.BlockSpec((tm, d), lambda i: (i, 0)), pl.BlockSpec((1, d), lambda i: (0, 0))],
        out_specs=pl.BlockSpec((tm, d), lambda i: (i, 0)),
        out_shape=jax.ShapeDtypeStruct((t, d), F32),
        compiler_params=_params(("arbitrary",), 32),
        name="final_norm",
    )(x, g)


def kernel(x, c, ada_w, ada_b, norm_mix_g, norm_ffn_g, w_in, b_f, conv_w, conv_b, conv_ln_g, conv_ln_b,
           conv_out, attn_out, w_out, ffn_w1, ffn_w3, ffn_w2, router_w, router_b, moe_w1, moe_w3, moe_w2,
           final_norm_g):
    bsz, seq, d = x.shape
    depth = ada_w.shape[0]
    d_conv = conv_w.shape[2]
    d_attn = attn_out.shape[1]
    t = bsz * seq
    moe_tm, moe_tf = 1024, 256
    final_g = final_norm_g.reshape(1, d)
    conv_out_b, attn_out_b, w_out_b = (w.astype(BF16) for w in (conv_out, attn_out, w_out))
    ffn_tf = 512
    n_dense, _, d_ff = ffn_w1.shape
    ffn_w1_b, ffn_w3_b = (w.astype(BF16).reshape(n_dense, d, d_ff // ffn_tf, ffn_tf).transpose(0, 2, 1, 3)
                          for w in (ffn_w1, ffn_w3))
    ffn_w2_b = ffn_w2.astype(BF16)

    mods = _adaln(c, ada_w, ada_b).reshape(depth, bsz, 6, d)
    xf = x.reshape(t, d)
    for l in range(depth):
        mod = mods[l]
        w_parts = _split_w_in(w_in[l], d_attn=d_attn, d_conv=d_conv)
        qkv, glu, gates, fg = _mix_in(xf, mod, norm_mix_g[l].reshape(1, d), w_parts, seq=seq)
        z = _conv_branch(glu, conv_w[l], conv_b[l], conv_ln_g[l], conv_ln_b[l], seq=seq)
        o = _fox_attn(qkv, _fox_gate(fg, b_f[l], seq=seq), seq=seq, d_attn=d_attn)
        xf = _mix_out(z, o, gates, conv_out_b, attn_out_b, w_out_b, l, xf, mod, seq=seq)
        g_ffn = norm_ffn_g[l].reshape(1, d)
        i = l // 2
        if l % 2 == 0:
            xf = _ffn(xf, mod, g_ffn, ffn_w1_b, ffn_w3_b, ffn_w2_b, i, seq=seq)
        else:
            h, idx, wts = _router(xf, mod, g_ffn, router_w[i], router_b[i], seq=seq)
            xrows = (moe_w1.shape[3] // moe_tf) * _moe_rows_per_step(moe_tm, moe_w1.shape[3], moe_tf)
            tables = _dispatch_tables(idx[:, :TOP_K], wts[:, :TOP_K], tm=moe_tm, xrows=xrows)
            ys = _moe_experts(h, tables, moe_w1, moe_w3, moe_w2, i, tm=moe_tm, tf=moe_tf)
            xf = _combine(xf, ys, mod, final_g, seq=seq, final=(l == depth - 1))
    if depth % 2 == 1:
        xf = _final_norm(xf, final_g)
    return xf.reshape(bsz, seq, d)
```

```python
import functools

import jax
import jax.numpy as jnp
from jax import lax
from jax.experimental import pallas as pl
from jax.experimental.pallas import tpu as pltpu

EPS = 1e-6
NEG_INF = -1e30
LOG2E = 1.4426950408889634
N_HEADS = 16
HEAD_DIM = 64
CONV_K = 31
N_EXPERTS = 8
TOP_K = 2

LANES = 128
CONV_HALO = 32
CONV_ROWS = 64
MIB = 1024 * 1024

F32 = jnp.float32
BF16 = jnp.bfloat16


def _params(semantics, vmem_mib):
    return pltpu.CompilerParams(dimension_semantics=semantics, vmem_limit_bytes=vmem_mib * MIB)


def _dot(a, b):
    return jnp.dot(a, b, preferred_element_type=F32)


def _silu(x):
    return x * jax.nn.sigmoid(x)


def _norm_mod(x, g, shift, scale):
    y = x * lax.rsqrt(jnp.mean(x * x, axis=-1, keepdims=True) + EPS) * g
    return y * (1.0 + scale) + shift


def _adaln_kernel(ct_ref, w_ref, b_ref, o_ref):
    @pl.when(pl.program_id(1) == 0)
    def _():
        o_ref[...] = jnp.broadcast_to(b_ref[...], o_ref.shape)

    w = w_ref[...]
    cols = _silu(ct_ref[...])
    for b in range(o_ref.shape[0]):
        o_ref[b:b + 1, :] += jnp.sum(w * cols[:, b:b + 1], axis=0, keepdims=True)


def _adaln(c, ada_w, ada_b, tk=128):
    depth, d, n = ada_w.shape
    bsz = c.shape[0]
    return pl.pallas_call(
        _adaln_kernel,
        grid=(depth, d // tk),
        in_specs=[pl.BlockSpec((tk, bsz), lambda l, k: (k, 0)),
                  pl.BlockSpec((None, tk, n), lambda l, k: (l, k, 0)),
                  pl.BlockSpec((None, 1, n), lambda l, k: (l, 0, 0))],
        out_specs=pl.BlockSpec((None, bsz, n), lambda l, k: (l, 0, 0)),
        out_shape=jax.ShapeDtypeStruct((depth, bsz, n), F32),
        compiler_params=_params(("arbitrary", "arbitrary"), 40),
        name="adaln",
    )(c.T, ada_w, ada_b.reshape(depth, 1, n))


def _mix_in_kernel(x_ref, mod_ref, g_ref, wqkv_ref, wa_ref, wg_ref, wgate_ref, wfg_ref,
                   qkv_ref, glu_ref, gate_ref, fg_ref, h_ref, *, n_q, n_qkv, n_glu):
    j = pl.program_id(1)

    @pl.when(j == 0)
    def _():
        h = _norm_mod(x_ref[...], g_ref[...], mod_ref[0:1, :], mod_ref[1:2, :]).astype(BF16)
        h_ref[...] = h
        fg_ref[...] = _dot(h, wfg_ref[...])

    @pl.when(j < n_q)
    def _():
        qkv_ref[...] = (_dot(h_ref[...], wqkv_ref[...]) * (HEAD_DIM ** -0.5 * LOG2E)).astype(BF16)

    @pl.when((j >= n_q) & (j < n_qkv))
    def _():
        qkv_ref[...] = _dot(h_ref[...], wqkv_ref[...]).astype(BF16)

    @pl.when((j >= n_qkv) & (j < n_qkv + n_glu))
    def _():
        h = h_ref[...]
        glu_ref[...] = (_dot(h, wa_ref[...]) * jax.nn.sigmoid(_dot(h, wg_ref[...]))).astype(BF16)

    @pl.when(j >= n_qkv + n_glu)
    def _():
        gate_ref[...] = jax.nn.sigmoid(_dot(h_ref[...], wgate_ref[...])).astype(BF16)


def _mix_in(x, mod, g, w_parts, *, seq, tm=1024, tn=512):
    t, d = x.shape
    w_qkv, w_a, w_g, w_gate, w_fg = w_parts
    d_attn, d_conv = w_qkv.shape[1] // 3, w_a.shape[1]
    n_q, n_qkv, n_glu, n_gate = d_attn // tn, 3 * d_attn // tn, 2 * d_conv // tn, 2 * d // tn
    tiles_per_batch = seq // tm
    glu_idx = lambda j: jnp.clip(j - n_qkv, 0, n_glu - 1)
    gate_idx = lambda j: jnp.clip(j - n_qkv - n_glu, 0, n_gate - 1)
    kern = functools.partial(_mix_in_kernel, n_q=n_q, n_qkv=n_qkv, n_glu=n_glu)
    return pl.pallas_call(
        kern,
        grid=(t // tm, n_qkv + n_glu + n_gate),
        in_specs=[pl.BlockSpec((tm, d), lambda i, j: (i, 0)),
                  pl.BlockSpec((None, 6, d), lambda i, j: (i // tiles_per_batch, 0, 0)),
                  pl.BlockSpec((1, d), lambda i, j: (0, 0)),
                  pl.BlockSpec((d, tn), lambda i, j: (0, jnp.minimum(j, n_qkv - 1))),
                  pl.BlockSpec((d, tn // 2), lambda i, j: (0, glu_idx(j))),
                  pl.BlockSpec((d, tn // 2), lambda i, j: (0, glu_idx(j))),
                  pl.BlockSpec((d, tn), lambda i, j: (0, gate_idx(j))),
                  pl.BlockSpec((d, LANES), lambda i, j: (0, 0))],
        out_specs=[pl.BlockSpec((tm, tn), lambda i, j: (i, jnp.minimum(j, n_qkv - 1))),
                   pl.BlockSpec((tm, tn // 2), lambda i, j: (i, glu_idx(j))),
                   pl.BlockSpec((tm, tn), lambda i, j: (i, gate_idx(j))),
                   pl.BlockSpec((tm, LANES), lambda i, j: (i, 0))],
        out_shape=[jax.ShapeDtypeStruct((t, 3 * d_attn), BF16),
                   jax.ShapeDtypeStruct((t, d_conv), BF16),
                   jax.ShapeDtypeStruct((t, 2 * d), BF16),
                   jax.ShapeDtypeStruct((t, LANES), F32)],
        scratch_shapes=[pltpu.VMEM((tm, d), BF16)],
        compiler_params=_params(("arbitrary", "arbitrary"), 48),
        name="mix_in",
    )(x, mod, g, w_qkv, w_a, w_g, w_gate, w_fg)


def _split_w_in(w, *, d_attn, d_conv):
    o = 3 * d_attn
    w_fg = jnp.pad(w[:, o:o + N_HEADS].astype(BF16), ((0, 0), (0, LANES - N_HEADS)))
    u = o + N_HEADS
    g = u + 2 * d_conv
    return (w[:, :o].astype(BF16), w[:, u:u + d_conv].astype(BF16), w[:, u + d_conv:g].astype(BF16),
            w[:, g:].astype(BF16), w_fg)


def _fox_gate_kernel(fg_ref, bf_ref, o_ref, carry_ref):
    @pl.when(pl.program_id(1) == 0)
    def _():
        carry_ref[...] = jnp.zeros_like(carry_ref)

    x = fg_ref[...] + bf_ref[...]
    logf = jnp.minimum(x, 0.0) - jnp.log(1.0 + jnp.exp(-jnp.abs(x)))
    n = x.shape[0]
    tri = (lax.broadcasted_iota(jnp.int32, (n, n), 1) <= lax.broadcasted_iota(jnp.int32, (n, n), 0)).astype(BF16)
    hi = logf.astype(BF16)
    r1 = logf - hi.astype(F32)
    mid = r1.astype(BF16)
    lo = (r1 - mid.astype(F32)).astype(BF16)
    cs = _dot(tri, hi) + _dot(tri, mid) + _dot(tri, lo) + carry_ref[...]
    o_ref[...] = cs
    carry_ref[...] = cs[n - 1:n, :]


def _fox_gate(fg, b_f, *, seq, chunk=256):
    t = fg.shape[0]
    bsz = t // seq
    bf = jnp.pad(b_f.astype(F32), (0, LANES - b_f.shape[0])).reshape(1, LANES)
    return pl.pallas_call(
        _fox_gate_kernel,
        grid=(bsz, seq // chunk),
        in_specs=[pl.BlockSpec((chunk, LANES), lambda b, i: (b * (seq // chunk) + i, 0)),
                  pl.BlockSpec((1, LANES), lambda b, i: (0, 0))],
        out_specs=pl.BlockSpec((chunk, LANES), lambda b, i: (b * (seq // chunk) + i, 0)),
        out_shape=jax.ShapeDtypeStruct((t, LANES), F32),
        scratch_shapes=[pltpu.VMEM((1, LANES), F32)],
        compiler_params=_params(("arbitrary", "arbitrary"), 16),
        name="fox_gate",
    )(fg, bf)


def _split3(x):
    hi = x.astype(BF16)
    r = x - hi.astype(F32)
    mid = r.astype(BF16)
    return hi, mid, (r - mid.astype(F32)).astype(BF16)


def _fox_attn_kernel(q_ref, k_ref, v_ref, f_ref, o_ref, kx_ref, vx_ref, *, tq):
    g = pl.program_id(1)
    qi = pl.program_id(2)
    n_heads = kx_ref.shape[0]
    lane = lax.broadcasted_iota(jnp.int32, (1, LANES), 1)
    first = lane < HEAD_DIM
    data = (first, jnp.logical_not(first))
    spare = (HEAD_DIM, 0)

    def pair_lanes(t):
        return slice((t // 2) * LANES, (t // 2 + 1) * LANES)

    @pl.when(qi == 0)
    def _():
        f_all = f_ref[...]
        for t in range(n_heads):
            k = k_ref[:, pair_lanes(t)].astype(F32)
            v = v_ref[:, pair_lanes(t)].astype(F32)
            f_key = jnp.sum(jnp.where(lane == n_heads * g + t, f_all, 0.0), axis=1, keepdims=True) * (-LOG2E)
            hi, mid, lo = (piece.astype(F32) for piece in _split3(f_key))
            s0 = spare[t % 2]
            bias = jnp.where(lane == s0, hi, jnp.where(lane == s0 + 1, mid, jnp.where(lane == s0 + 2, lo, 0.0)))
            kx_ref[t] = jnp.where(data[t % 2], k, bias).astype(BF16)
            vx_ref[t] = jnp.where(data[t % 2], v, jnp.where(lane == s0, 1.0, 0.0)).astype(BF16)

    f_rows = f_ref[pl.ds(pl.multiple_of(qi * tq, tq), tq), :]
    q_h, fq_h = [], []
    for t in range(n_heads):
        q = q_ref[:, pair_lanes(t)].astype(F32)
        s0 = spare[t % 2]
        ones = (lane >= s0) & (lane < s0 + 3)
        q_h.append(jnp.where(data[t % 2], q, jnp.where(ones, 1.0, 0.0)).astype(BF16))
        fq_h.append(jnp.sum(jnp.where(lane == n_heads * g + t, f_rows, 0.0), axis=1, keepdims=True) * LOG2E)
    causal = lax.broadcasted_iota(jnp.int32, (tq, tq), 1) <= lax.broadcasted_iota(jnp.int32, (tq, tq), 0)

    def step(j, carry, masked):
        off = pl.multiple_of(j * tq, tq)
        out = []
        for t in range(n_heads):
            m, acc = carry[t]
            s = lax.dot_general(q_h[t], kx_ref[t, pl.ds(off, tq), :], (((1,), (1,)), ((), ())),
                                preferred_element_type=F32)
            if masked:
                s = jnp.where(causal, s, NEG_INF)
            m_new = jnp.maximum(m, jnp.max(s, axis=1, keepdims=True) + fq_h[t])
            pexp = jnp.exp2(s + (fq_h[t] - m_new))
            acc = jnp.exp2(m - m_new) * acc + _dot(pexp.astype(BF16), vx_ref[t, pl.ds(off, tq), :])
            out.append((m_new, acc))
        return tuple(out)

    init = tuple((jnp.full((tq, 1), NEG_INF, F32), jnp.zeros((tq, LANES), F32)) for _ in range(n_heads))
    carry = lax.fori_loop(0, qi, lambda j, c: step(j, c, False), init)
    accs = [acc for _, acc in step(qi, carry, True)]
    for t in range(0, n_heads, 2):
        out_a = accs[t] / accs[t][:, spare[0]:spare[0] + 1]
        out_b = accs[t + 1] / accs[t + 1][:, spare[1]:spare[1] + 1]
        o_ref[:, pair_lanes(t)] = jnp.where(first, out_a, out_b).astype(o_ref.dtype)


def _fox_attn(qkv, f_cum, *, seq, d_attn, tq=512, heads=4):
    t = qkv.shape[0]
    bsz = t // seq
    width = heads * HEAD_DIM
    n_groups = d_attn // width
    nq = seq // tq
    kern = functools.partial(_fox_attn_kernel, tq=tq)
    return pl.pallas_call(
        kern,
        grid=(bsz, n_groups, nq),
        in_specs=[pl.BlockSpec((tq, width), lambda b, g, i: (b * nq + i, g)),
                  pl.BlockSpec((seq, width), lambda b, g, i: (b, n_groups + g)),
                  pl.BlockSpec((seq, width), lambda b, g, i: (b, 2 * n_groups + g)),
                  pl.BlockSpec((seq, LANES), lambda b, g, i: (b, 0))],
        out_specs=pl.BlockSpec((tq, width), lambda b, g, i: (b * nq + i, g)),
        out_shape=jax.ShapeDtypeStruct((t, d_attn), BF16),
        scratch_shapes=[pltpu.VMEM((heads, seq, LANES), BF16), pltpu.VMEM((heads, seq, LANES), BF16)],
        compiler_params=_params(("arbitrary", "arbitrary", "arbitrary"), 48),
        name="fox_attn",
    )(qkv, qkv, qkv, f_cum)


def _conv_kernel(cur_ref, halo_ref, w_ref, cb_ref, lg_ref, lb_ref, o_ref, ybuf, cbuf, *, ts):
    qi = pl.program_id(1)
    c = cur_ref.shape[1]
    halo = halo_ref[...].astype(F32)
    ybuf[0:CONV_HALO, :] = jnp.where(qi > 0, halo, 0.0)
    ybuf[CONV_HALO:CONV_HALO + ts, :] = cur_ref[...].astype(F32)

    def lane_chunk(ci, _):
        lo = pl.multiple_of(ci * LANES, LANES)
        for r in range(ts // CONV_ROWS):
            acc = jnp.broadcast_to(cb_ref[:, pl.ds(lo, LANES)], (CONV_ROWS, LANES))
            for j in range(CONV_K):
                start = r * CONV_ROWS + CONV_HALO - (CONV_K - 1) + j
                acc = acc + w_ref[j:j + 1, pl.ds(lo, LANES)] * ybuf[start:start + CONV_ROWS, pl.ds(lo, LANES)]
            cbuf[r * CONV_ROWS:(r + 1) * CONV_ROWS, pl.ds(lo, LANES)] = acc
        return 0

    lax.fori_loop(0, c // LANES, lane_chunk, 0)
    y = cbuf[...]
    mu = jnp.mean(y, axis=-1, keepdims=True)
    yc = y - mu
    z = yc * lax.rsqrt(jnp.mean(yc * yc, axis=-1, keepdims=True) + EPS) * lg_ref[...] + lb_ref[...]
    o_ref[...] = _silu(z).astype(o_ref.dtype)


def _conv_branch(y, conv_w, conv_b, ln_g, ln_b, *, seq, ts=256):
    t, c = y.shape
    bsz = t // seq
    ns = seq // ts
    w = jnp.pad(conv_w.astype(F32), ((0, CONV_HALO - CONV_K), (0, 0)))
    kern = functools.partial(_conv_kernel, ts=ts)
    row = lambda a: a.astype(F32).reshape(1, c)
    return pl.pallas_call(
        kern,
        grid=(bsz, ns),
        in_specs=[pl.BlockSpec((ts, c), lambda b, i: (b * ns + i, 0)),
                  pl.BlockSpec((CONV_HALO, c),
                               lambda b, i: (jnp.maximum((b * ns + i) * (ts // CONV_HALO) - 1, 0), 0)),
                  pl.BlockSpec((CONV_HALO, c), lambda b, i: (0, 0)),
                  pl.BlockSpec((1, c), lambda b, i: (0, 0)),
                  pl.BlockSpec((1, c), lambda b, i: (0, 0)),
                  pl.BlockSpec((1, c), lambda b, i: (0, 0))],
        out_specs=pl.BlockSpec((ts, c), lambda b, i: (b * ns + i, 0)),
        out_shape=jax.ShapeDtypeStruct((t, c), BF16),
        scratch_shapes=[pltpu.VMEM((CONV_HALO + ts, c), F32), pltpu.VMEM((ts, c), F32)],
        compiler_params=_params(("arbitrary", "arbitrary"), 32),
        name="conv_branch",
    )(y, y, w, row(conv_b), row(ln_g), row(ln_b))


def _mix_out_kernel(z_ref, o_ref, sga_ref, sgb_ref, co_ref, ao_ref, wo_ref, x_ref, mod_ref, out_ref, acc_ref):
    j = pl.program_id(1)
    y_a = _dot(z_ref[...], co_ref[...])
    y_b = _dot(o_ref[...], ao_ref[...])
    merged = (sga_ref[...].astype(F32) * y_a + sgb_ref[...].astype(F32) * y_b).astype(BF16)

    @pl.when(j == 0)
    def _():
        acc_ref[...] = jnp.zeros_like(acc_ref)

    acc_ref[...] += _dot(merged, wo_ref[...])

    @pl.when(j == pl.num_programs(1) - 1)
    def _():
        out_ref[...] = x_ref[...] + mod_ref[2:3, :] * acc_ref[...]


def _mix_out(z, o, gates, conv_out, attn_out, w_out, layer, x, mod, *, seq, tm=512, tn=1024):
    t, d = x.shape
    c = z.shape[1]
    nj = d // tn
    tiles_per_batch = seq // tm
    return pl.pallas_call(
        _mix_out_kernel,
        grid=(t // tm, nj),
        in_specs=[pl.BlockSpec((tm, c), lambda i, j: (i, 0)),
                  pl.BlockSpec((tm, c), lambda i, j: (i, 0)),
                  pl.BlockSpec((tm, tn), lambda i, j: (i, j)),
                  pl.BlockSpec((tm, tn), lambda i, j: (i, nj + j)),
                  pl.BlockSpec((None, c, tn), lambda i, j: (layer, 0, j)),
                  pl.BlockSpec((None, c, tn), lambda i, j: (layer, 0, j)),
                  pl.BlockSpec((None, tn, d), lambda i, j: (layer, j, 0)),
                  pl.BlockSpec((tm, d), lambda i, j: (i, 0)),
                  pl.BlockSpec((None, 6, d), lambda i, j: (i // tiles_per_batch, 0, 0))],
        out_specs=pl.BlockSpec((tm, d), lambda i, j: (i, 0)),
        out_shape=jax.ShapeDtypeStruct((t, d), F32),
        scratch_shapes=[pltpu.VMEM((tm, d), F32)],
        compiler_params=_params(("arbitrary", "arbitrary"), 48),
        name="mix_out",
    )(z, o, gates, gates, conv_out, attn_out, w_out, x, mod)


def _ffn_kernel(x_ref, mod_ref, g_ref, w1_ref, w3_ref, w2_ref, out_ref, h_ref, acc_ref):
    j = pl.program_id(1)

    @pl.when(j == 0)
    def _():
        h_ref[...] = _norm_mod(x_ref[...], g_ref[...], mod_ref[3:4, :], mod_ref[4:5, :]).astype(BF16)
        acc_ref[...] = jnp.zeros_like(acc_ref)

    h = h_ref[...]
    hidden = (_silu(_dot(h, w1_ref[...])) * _dot(h, w3_ref[...])).astype(BF16)
    acc_ref[...] += _dot(hidden, w2_ref[...])

    @pl.when(j == pl.num_programs(1) - 1)
    def _():
        out_ref[...] = x_ref[...] + mod_ref[5:6, :] * acc_ref[...]


def _ffn(x, mod, g, w1, w3, w2, layer, *, seq, tm=512, tf=512):
    t, d = x.shape
    f = w1.shape[2]
    tiles_per_batch = seq // tm
    return pl.pallas_call(
        _ffn_kernel,
        grid=(t // tm, f // tf),
        in_specs=[pl.BlockSpec((tm, d), lambda i, j: (i, 0)),
                  pl.BlockSpec((None, 6, d), lambda i, j: (i // tiles_per_batch, 0, 0)),
                  pl.BlockSpec((1, d), lambda i, j: (0, 0)),
                  pl.BlockSpec((None, d, tf), lambda i, j: (layer, 0, j)),
                  pl.BlockSpec((None, d, tf), lambda i, j: (layer, 0, j)),
                  pl.BlockSpec((None, tf, d), lambda i, j: (layer, j, 0))],
        out_specs=pl.BlockSpec((tm, d), lambda i, j: (i, 0)),
        out_shape=jax.ShapeDtypeStruct((t, d), F32),
        scratch_shapes=[pltpu.VMEM((tm, d), BF16), pltpu.VMEM((tm, d), F32)],
        compiler_params=_params(("arbitrary", "arbitrary"), 48),
        name="ffn",
    )(x, mod, g, w1, w3, w2)


def _router_kernel(x_ref, mod_ref, g_ref, rw_ref, rb_ref, h_ref, idx_ref, wt_ref):
    h = _norm_mod(x_ref[...], g_ref[...], mod_ref[3:4, :], mod_ref[4:5, :])
    h_ref[...] = h
    logits = jnp.dot(h, rw_ref[...], preferred_element_type=F32, precision=lax.Precision.HIGHEST) + rb_ref[...]
    lane = lax.broadcasted_iota(jnp.int32, logits.shape, 1).astype(F32)
    lg = jnp.where(lane < N_EXPERTS, logits, -jnp.inf)
    v1 = jnp.max(lg, axis=1, keepdims=True)
    i1 = jnp.min(jnp.where(lg == v1, lane, float(LANES)), axis=1, keepdims=True)
    lg2 = jnp.where(lane == i1, -jnp.inf, lg)
    v2 = jnp.max(lg2, axis=1, keepdims=True)
    i2 = jnp.min(jnp.where(lg2 == v2, lane, float(LANES)), axis=1, keepdims=True)
    e2 = jnp.exp(v2 - v1)
    w1 = 1.0 / (1.0 + e2)
    w2 = e2 / (1.0 + e2)
    idx_ref[...] = jnp.where(lane == 0.0, i1, jnp.where(lane == 1.0, i2, 0.0)).astype(jnp.int32)
    wt_ref[...] = jnp.where(lane == 0.0, w1, jnp.where(lane == 1.0, w2, 0.0))


def _router(x, mod, g, router_w, router_b, *, seq, tm=512):
    t, d = x.shape
    tiles_per_batch = seq // tm
    rw = jnp.pad(router_w.astype(F32), ((0, 0), (0, LANES - N_EXPERTS)))
    rb = jnp.pad(router_b.astype(F32), (0, LANES - N_EXPERTS)).reshape(1, LANES)
    return pl.pallas_call(
        _router_kernel,
        grid=(t // tm,),
        in_specs=[pl.BlockSpec((tm, d), lambda i: (i, 0)),
                  pl.BlockSpec((None, 6, d), lambda i: (i // tiles_per_batch, 0, 0)),
                  pl.BlockSpec((1, d), lambda i: (0, 0)),
                  pl.BlockSpec((d, LANES), lambda i: (0, 0)),
                  pl.BlockSpec((1, LANES), lambda i: (0, 0))],
        out_specs=[pl.BlockSpec((tm, d), lambda i: (i, 0)),
                   pl.BlockSpec((tm, LANES), lambda i: (i, 0)),
                   pl.BlockSpec((tm, LANES), lambda i: (i, 0))],
        out_shape=[jax.ShapeDtypeStruct((t, d), F32),
                   jax.ShapeDtypeStruct((t, LANES), jnp.int32),
                   jax.ShapeDtypeStruct((t, LANES), F32)],
        compiler_params=_params(("arbitrary",), 40),
        name="router",
    )(x, mod, g, rw, rb)


def _dispatch_tables(idx, wts, *, tm, xrows):
    t = idx.shape[0]
    n_rows = TOP_K * t + N_EXPERTS * tm
    n_tiles = n_rows // tm
    flat_e = idx.reshape(-1)
    onehot = (flat_e[:, None] == jnp.arange(N_EXPERTS, dtype=jnp.int32)[None, :]).astype(jnp.int32)
    rank = jnp.sum((jnp.cumsum(onehot, axis=0) - onehot) * onehot, axis=1)
    counts = jnp.sum(onehot, axis=0)
    padded = ((counts + tm - 1) // tm) * tm
    ends = jnp.cumsum(padded)
    starts = ends - padded
    pos = starts[flat_e] + rank
    row_src = jnp.full((n_rows,), -1, jnp.int32).at[pos].set(jnp.arange(TOP_K * t, dtype=jnp.int32))
    valid = row_src >= 0
    src = jnp.maximum(row_src, 0)
    row_token = src // TOP_K
    row_dst = jnp.where(valid, (src % TOP_K) * t + row_token, -1)
    row_gate = jnp.where(valid, wts.reshape(-1)[src], 0.0).reshape(n_rows, 1)
    n_active = (ends[-1] // tm).astype(jnp.int32)
    tile_start = jnp.minimum(jnp.arange(n_tiles, dtype=jnp.int32), n_active - 1) * tm
    tile_expert = jnp.minimum(jnp.sum((tile_start[:, None] >= ends[None, :]).astype(jnp.int32), axis=1),
                              N_EXPERTS - 1)
    tile_rows = jnp.sum(valid.reshape(n_tiles, tm).astype(jnp.int32), axis=1)
    tile_token = jnp.pad(row_token.reshape(n_tiles, tm), ((0, 1), (0, xrows - tm))).reshape(-1)
    spare = TOP_K * t + jnp.arange(xrows, dtype=jnp.int32)
    dst = jnp.pad(row_dst.reshape(n_tiles, tm), ((0, 0), (0, xrows - tm)), constant_values=-1)
    dst = jnp.where(dst >= 0, dst, spare[None, :])
    prev_dst = jnp.concatenate([spare[None, :], dst], axis=0).reshape(-1)
    return tile_expert, n_active.reshape(1), tile_token, prev_dst, tile_rows, row_gate


def _moe_kernel(te_ref, na_ref, tok_ref, dst_ref, rows_ref, h_hbm, gate_ref, w1_ref, w3_ref, w2_ref, ys_hbm,
                xbuf, xbf, acc_ref, obuf, sem_in, sem_out, *, tm, sub, per_step):
    i = pl.program_id(0)
    j = pl.program_id(1)
    last_j = pl.num_programs(1) - 1
    n_active = na_ref[0]
    active = i < n_active
    xrows = xbuf.shape[0]

    def gather_row(r, tok):
        return pltpu.make_async_copy(h_hbm.at[pl.ds(tok, 1)], xbuf.at[pl.ds(r, 1)], sem_in)

    def scatter_row(r, dst):
        return pltpu.make_async_copy(obuf.at[pl.ds(r, 1)], ys_hbm.at[pl.ds(dst, 1)], sem_out)

    def wait_gathered():
        pltpu.make_async_copy(h_hbm.at[pl.ds(0, xrows)], xbuf, sem_in).wait()

    @pl.when(active & (j == 0))
    def _():
        @pl.when(i == 0)
        def _():
            obuf[...] = jnp.zeros_like(obuf)

            def issue(r, _):
                gather_row(r, tok_ref[r]).start()
                return 0

            lax.fori_loop(0, xrows, issue, 0, unroll=8)

        wait_gathered()
        xbf[...] = xbuf[0:tm, :].astype(BF16)

    @pl.when(active)
    def _():
        w1 = w1_ref[...].astype(BF16)
        w3 = w3_ref[...].astype(BF16)
        w2 = w2_ref[...].astype(BF16)
        n_valid = rows_ref[i]
        for s in range(tm // sub):
            rows = slice(s * sub, (s + 1) * sub)

            def compute(rows=rows, s=s):
                @pl.when(j == 0)
                def _():
                    acc_ref[rows, :] = jnp.zeros((sub, acc_ref.shape[1]), F32)

                x = xbf[rows, :]
                hidden = (_silu(_dot(x, w1)) * _dot(x, w3)).astype(BF16)
                acc_ref[rows, :] += _dot(hidden, w2)

                if s == 0:
                    for u in range(per_step):
                        r = j * per_step + u
                        gather_row(r, tok_ref[(i + 1) * xrows + r]).start()
                        scatter_row(r, dst_ref[i * xrows + r]).start()

                @pl.when(j == last_j)
                def _():
                    if s == 0:
                        pltpu.make_async_copy(obuf, ys_hbm.at[pl.ds(0, xrows)], sem_out).wait()
                    obuf[rows, :] = acc_ref[rows, :] * gate_ref[rows, :]

            if s == 0:
                compute()
            else:
                pl.when(n_valid > s * sub)(compute)

    @pl.when(active & (j == last_j) & (i == n_active - 1))
    def _():
        n_valid = rows_ref[i]
        base = (i + 1) * xrows

        def issue(r, _):
            scatter_row(r, dst_ref[base + r]).start()
            return 0

        def wait(r, _):
            scatter_row(0, 0).wait()
            return 0

        lax.fori_loop(0, n_valid, issue, 0)
        wait_gathered()
        lax.fori_loop(0, n_valid, wait, 0)


def _moe_rows_per_step(tm, f, tf):
    nf = f // tf
    return -(-tm // (8 * nf)) * 8


def _moe_experts(h, tables, w1, w3, w2, layer, *, tm, sub=512, tf=256):
    t, d = h.shape
    f = w1.shape[3]
    nf = f // tf
    assert nf >= 2 and tm % sub == 0
    per_step = _moe_rows_per_step(tm, f, tf)
    xrows = nf * per_step
    tile_expert, n_active, tile_token, prev_dst, tile_rows, row_gate = tables
    n_tiles = tile_expert.shape[0]
    assert tile_token.shape[0] == (n_tiles + 1) * xrows
    kern = functools.partial(_moe_kernel, tm=tm, sub=sub, per_step=per_step)

    def f_idx(i, j, na):
        return jnp.where(i < na[0], j, nf - 1)

    grid_spec = pltpu.PrefetchScalarGridSpec(
        num_scalar_prefetch=5,
        grid=(n_tiles, nf),
        in_specs=[pl.BlockSpec(memory_space=pl.ANY),
                  pl.BlockSpec((tm, 1), lambda i, j, te, na, *_: (i, 0)),
                  pl.BlockSpec((None, None, d, tf), lambda i, j, te, na, *_: (layer, te[i], 0, f_idx(i, j, na))),
                  pl.BlockSpec((None, None, d, tf), lambda i, j, te, na, *_: (layer, te[i], 0, f_idx(i, j, na))),
                  pl.BlockSpec((None, None, tf, d), lambda i, j, te, na, *_: (layer, te[i], f_idx(i, j, na), 0))],
        out_specs=pl.BlockSpec(memory_space=pl.ANY),
        scratch_shapes=[pltpu.VMEM((xrows, d), F32), pltpu.VMEM((tm, d), BF16), pltpu.VMEM((tm, d), F32),
                        pltpu.VMEM((xrows, d), F32), pltpu.SemaphoreType.DMA(()), pltpu.SemaphoreType.DMA(())],
    )
    return pl.pallas_call(
        kern,
        grid_spec=grid_spec,
        out_shape=jax.ShapeDtypeStruct((TOP_K * t + xrows, d), F32),
        compiler_params=_params(("arbitrary", "arbitrary"), 58),
        name="moe_experts",
    )(tile_expert, n_active, tile_token, prev_dst, tile_rows, h, row_gate, w1, w3, w2)


def _combine_kernel(x_ref, y0_ref, y1_ref, mod_ref, g_ref, out_ref, *, final):
    x = x_ref[...] + mod_ref[5:6, :] * (y0_ref[...] + y1_ref[...])
    if final:
        x = x * lax.rsqrt(jnp.mean(x * x, axis=-1, keepdims=True) + EPS) * g_ref[...]
    out_ref[...] = x


def _combine(x, ys, mod, final_g, *, seq, final, tm=512):
    t, d = x.shape
    nt = t // tm
    tiles_per_batch = seq // tm
    return pl.pallas_call(
        functools.partial(_combine_kernel, final=final),
        grid=(nt,),
        in_specs=[pl.BlockSpec((tm, d), lambda i: (i, 0)),
                  pl.BlockSpec((tm, d), lambda i: (i, 0)),
                  pl.BlockSpec((tm, d), lambda i: (nt + i, 0)),
                  pl.BlockSpec((None, 6, d), lambda i: (i // tiles_per_batch, 0, 0)),
                  pl.BlockSpec((1, d), lambda i: (0, 0))],
        out_specs=pl.BlockSpec((tm, d), lambda i: (i, 0)),
        out_shape=jax.ShapeDtypeStruct((t, d), F32),
        compiler_params=_params(("arbitrary",), 48),
        name="moe_combine",
    )(x, ys, ys, mod, final_g)


def _final_norm_kernel(x_ref, g_ref, o_ref):
    x = x_ref[...]
    o_ref[...] = x * lax.rsqrt(jnp.mean(x * x, axis=-1, keepdims=True) + EPS) * g_ref[...]


def _final_norm(x, g, tm=512):
    t, d = x.shape
    return pl.pallas_call(
        _final_norm_kernel,
        grid=(t // tm,),
        in_specs=[pl.BlockSpec((tm, d), lambda i: (i, 0)), pl.BlockSpec((1, d), lambda i: (0, 0))],
        out_specs=pl.BlockSpec((tm, d), lambda i: (i, 0)),
        out_shape=jax.ShapeDtypeStruct((t, d), F32),
        compiler_params=_params(("arbitrary",), 32),
        name="final_norm",
    )(x, g)


def kernel(x, c, ada_w, ada_b, norm_mix_g, norm_ffn_g, w_in, b_f, conv_w, conv_b, conv_ln_g, conv_ln_b,
           conv_out, attn_out, w_out, ffn_w1, ffn_w3, ffn_w2, router_w, router_b, moe_w1, moe_w3, moe_w2,
           final_norm_g):
    bsz, seq, d = x.shape
    depth = ada_w.shape[0]
    d_conv = conv_w.shape[2]
    d_attn = attn_out.shape[1]
    t = bsz * seq
    moe_tm, moe_tf = 1024, 256
    final_g = final_norm_g.reshape(1, d)
    conv_out_b, attn_out_b, w_out_b = (w.astype(BF16) for w in (conv_out, attn_out, w_out))
    ffn_w1_b, ffn_w3_b, ffn_w2_b = (w.astype(BF16) for w in (ffn_w1, ffn_w3, ffn_w2))

    mods = _adaln(c, ada_w, ada_b).reshape(depth, bsz, 6, d)
    xf = x.reshape(t, d)
    for l in range(depth):
        mod = mods[l]
        w_parts = _split_w_in(w_in[l], d_attn=d_attn, d_conv=d_conv)
        qkv, glu, gates, fg = _mix_in(xf, mod, norm_mix_g[l].reshape(1, d), w_parts, seq=seq)
        z = _conv_branch(glu, conv_w[l], conv_b[l], conv_ln_g[l], conv_ln_b[l], seq=seq)
        o = _fox_attn(qkv, _fox_gate(fg, b_f[l], seq=seq), seq=seq, d_attn=d_attn)
        xf = _mix_out(z, o, gates, conv_out_b, attn_out_b, w_out_b, l, xf, mod, seq=seq)
        g_ffn = norm_ffn_g[l].reshape(1, d)
        i = l // 2
        if l % 2 == 0:
            xf = _ffn(xf, mod, g_ffn, ffn_w1_b, ffn_w3_b, ffn_w2_b, i, seq=seq)
        else:
            h, idx, wts = _router(xf, mod, g_ffn, router_w[i], router_b[i], seq=seq)
            xrows = (moe_w1.shape[3] // moe_tf) * _moe_rows_per_step(moe_tm, moe_w1.shape[3], moe_tf)
            tables = _dispatch_tables(idx[:, :TOP_K], wts[:, :TOP_K], tm=moe_tm, xrows=xrows)
            ys = _moe_experts(h, tables, moe_w1, moe_w3, moe_w2, i, tm=moe_tm, tf=moe_tf)
            xf = _combine(xf, ys, mod, final_g, seq=seq, final=(l == depth - 1))
    if depth % 2 == 1:
        xf = _final_norm(xf, final_g)
    return xf.reshape(bsz, seq, d)
```

```python
import functools

import jax
import jax.numpy as jnp
from jax import lax
from jax.experimental import pallas as pl
from jax.experimental.pallas import tpu as pltpu

EPS = 1e-6
NEG_INF = -1e30
LOG2E = 1.4426950408889634
N_HEADS = 16
HEAD_DIM = 64
CONV_K = 31
N_EXPERTS = 8
TOP_K = 2

LANES = 128
CONV_HALO = 32
CONV_ROWS = 64
MIB = 1024 * 1024

F32 = jnp.float32
BF16 = jnp.bfloat16


def _params(semantics, vmem_mib):
    return pltpu.CompilerParams(dimension_semantics=semantics, vmem_limit_bytes=vmem_mib * MIB)


def _dot(a, b):
    return jnp.dot(a, b, preferred_element_type=F32)


def _silu(x):
    return x * jax.nn.sigmoid(x)


def _norm_mod(x, g, shift, scale):
    y = x * lax.rsqrt(jnp.mean(x * x, axis=-1, keepdims=True) + EPS) * g
    return y * (1.0 + scale) + shift


def _adaln_kernel(ct_ref, w_ref, b_ref, o_ref):
    @pl.when(pl.program_id(1) == 0)
    def _():
        o_ref[...] = jnp.broadcast_to(b_ref[...], o_ref.shape)

    w = w_ref[...]
    cols = _silu(ct_ref[...])
    for b in range(o_ref.shape[0]):
        o_ref[b:b + 1, :] += jnp.sum(w * cols[:, b:b + 1], axis=0, keepdims=True)


def _adaln(c, ada_w, ada_b, tk=128):
    depth, d, n = ada_w.shape
    bsz = c.shape[0]
    return pl.pallas_call(
        _adaln_kernel,
        grid=(depth, d // tk),
        in_specs=[pl.BlockSpec((tk, bsz), lambda l, k: (k, 0)),
                  pl.BlockSpec((None, tk, n), lambda l, k: (l, k, 0)),
                  pl.BlockSpec((None, 1, n), lambda l, k: (l, 0, 0))],
        out_specs=pl.BlockSpec((None, bsz, n), lambda l, k: (l, 0, 0)),
        out_shape=jax.ShapeDtypeStruct((depth, bsz, n), F32),
        compiler_params=_params(("arbitrary", "arbitrary"), 40),
        name="adaln",
    )(c.T, ada_w, ada_b.reshape(depth, 1, n))


def _mix_in_kernel(x_ref, mod_ref, g_ref, wqkv_ref, wa_ref, wg_ref, wgate_ref, wfg_ref,
                   qkv_ref, glu_ref, gate_ref, fg_ref, h_ref, *, n_q, n_qkv, n_glu):
    j = pl.program_id(1)

    @pl.when(j == 0)
    def _():
        h = _norm_mod(x_ref[...], g_ref[...], mod_ref[0:1, :], mod_ref[1:2, :]).astype(BF16)
        h_ref[...] = h
        fg_ref[...] = _dot(h, wfg_ref[...])

    @pl.when(j < n_q)
    def _():
        qkv_ref[...] = (_dot(h_ref[...], wqkv_ref[...]) * (HEAD_DIM ** -0.5 * LOG2E)).astype(BF16)

    @pl.when((j >= n_q) & (j < n_qkv))
    def _():
        qkv_ref[...] = _dot(h_ref[...], wqkv_ref[...]).astype(BF16)

    @pl.when((j >= n_qkv) & (j < n_qkv + n_glu))
    def _():
        h = h_ref[...]
        glu_ref[...] = (_dot(h, wa_ref[...]) * jax.nn.sigmoid(_dot(h, wg_ref[...]))).astype(BF16)

    @pl.when(j >= n_qkv + n_glu)
    def _():
        gate_ref[...] = jax.nn.sigmoid(_dot(h_ref[...], wgate_ref[...])).astype(BF16)


def _mix_in(x, mod, g, w_parts, *, seq, tm=1024, tn=512):
    t, d = x.shape
    w_qkv, w_a, w_g, w_gate, w_fg = w_parts
    d_attn, d_conv = w_qkv.shape[1] // 3, w_a.shape[1]
    n_q, n_qkv, n_glu, n_gate = d_attn // tn, 3 * d_attn // tn, 2 * d_conv // tn, 2 * d // tn
    tiles_per_batch = seq // tm
    glu_idx = lambda j: jnp.clip(j - n_qkv, 0, n_glu - 1)
    gate_idx = lambda j: jnp.clip(j - n_qkv - n_glu, 0, n_gate - 1)
    kern = functools.partial(_mix_in_kernel, n_q=n_q, n_qkv=n_qkv, n_glu=n_glu)
    return pl.pallas_call(
        kern,
        grid=(t // tm, n_qkv + n_glu + n_gate),
        in_specs=[pl.BlockSpec((tm, d), lambda i, j: (i, 0)),
                  pl.BlockSpec((None, 6, d), lambda i, j: (i // tiles_per_batch, 0, 0)),
                  pl.BlockSpec((1, d), lambda i, j: (0, 0)),
                  pl.BlockSpec((d, tn), lambda i, j: (0, jnp.minimum(j, n_qkv - 1))),
                  pl.BlockSpec((d, tn // 2), lambda i, j: (0, glu_idx(j))),
                  pl.BlockSpec((d, tn // 2), lambda i, j: (0, glu_idx(j))),
                  pl.BlockSpec((d, tn), lambda i, j: (0, gate_idx(j))),
                  pl.BlockSpec((d, LANES), lambda i, j: (0, 0))],
        out_specs=[pl.BlockSpec((tm, tn), lambda i, j: (i, jnp.minimum(j, n_qkv - 1))),
                   pl.BlockSpec((tm, tn // 2), lambda i, j: (i, glu_idx(j))),
                   pl.BlockSpec((tm, tn), lambda i, j: (i, gate_idx(j))),
                   pl.BlockSpec((tm, LANES), lambda i, j: (i, 0))],
        out_shape=[jax.ShapeDtypeStruct((t, 3 * d_attn), BF16),
                   jax.ShapeDtypeStruct((t, d_conv), BF16),
                   jax.ShapeDtypeStruct((t, 2 * d), BF16),
                   jax.ShapeDtypeStruct((t, LANES), F32)],
        scratch_shapes=[pltpu.VMEM((tm, d), BF16)],
        compiler_params=_params(("arbitrary", "arbitrary"), 48),
        name="mix_in",
    )(x, mod, g, w_qkv, w_a, w_g, w_gate, w_fg)


def _split_w_in(w, *, d_attn, d_conv):
    o = 3 * d_attn
    w_fg = jnp.pad(w[:, o:o + N_HEADS].astype(BF16), ((0, 0), (0, LANES - N_HEADS)))
    u = o + N_HEADS
    g = u + 2 * d_conv
    return (w[:, :o].astype(BF16), w[:, u:u + d_conv].astype(BF16), w[:, u + d_conv:g].astype(BF16),
            w[:, g:].astype(BF16), w_fg)


def _fox_gate_kernel(fg_ref, bf_ref, o_ref, carry_ref):
    @pl.when(pl.program_id(1) == 0)
    def _():
        carry_ref[...] = jnp.zeros_like(carry_ref)

    x = fg_ref[...] + bf_ref[...]
    logf = jnp.minimum(x, 0.0) - jnp.log(1.0 + jnp.exp(-jnp.abs(x)))
    n = x.shape[0]
    tri = (lax.broadcasted_iota(jnp.int32, (n, n), 1) <= lax.broadcasted_iota(jnp.int32, (n, n), 0)).astype(BF16)
    hi = logf.astype(BF16)
    r1 = logf - hi.astype(F32)
    mid = r1.astype(BF16)
    lo = (r1 - mid.astype(F32)).astype(BF16)
    cs = _dot(tri, hi) + _dot(tri, mid) + _dot(tri, lo) + carry_ref[...]
    o_ref[...] = cs
    carry_ref[...] = cs[n - 1:n, :]


def _fox_gate(fg, b_f, *, seq, chunk=256):
    t = fg.shape[0]
    bsz = t // seq
    bf = jnp.pad(b_f.astype(F32), (0, LANES - b_f.shape[0])).reshape(1, LANES)
    return pl.pallas_call(
        _fox_gate_kernel,
        grid=(bsz, seq // chunk),
        in_specs=[pl.BlockSpec((chunk, LANES), lambda b, i: (b * (seq // chunk) + i, 0)),
                  pl.BlockSpec((1, LANES), lambda b, i: (0, 0))],
        out_specs=pl.BlockSpec((chunk, LANES), lambda b, i: (b * (seq // chunk) + i, 0)),
        out_shape=jax.ShapeDtypeStruct((t, LANES), F32),
        scratch_shapes=[pltpu.VMEM((1, LANES), F32)],
        compiler_params=_params(("arbitrary", "arbitrary"), 16),
        name="fox_gate",
    )(fg, bf)


def _split3(x):
    hi = x.astype(BF16)
    r = x - hi.astype(F32)
    mid = r.astype(BF16)
    return hi, mid, (r - mid.astype(F32)).astype(BF16)


def _fox_attn_kernel(q_ref, k_ref, v_ref, f_ref, o_ref, kx_ref, vx_ref, *, tq):
    g = pl.program_id(1)
    qi = pl.program_id(2)
    n_heads = kx_ref.shape[0]
    lane = lax.broadcasted_iota(jnp.int32, (1, LANES), 1)
    first = lane < HEAD_DIM
    data = (first, jnp.logical_not(first))
    spare = (HEAD_DIM, 0)

    def pair_lanes(t):
        return slice((t // 2) * LANES, (t // 2 + 1) * LANES)

    @pl.when(qi == 0)
    def _():
        f_all = f_ref[...]
        for t in range(n_heads):
            k = k_ref[:, pair_lanes(t)].astype(F32)
            v = v_ref[:, pair_lanes(t)].astype(F32)
            f_key = jnp.sum(jnp.where(lane == n_heads * g + t, f_all, 0.0), axis=1, keepdims=True) * (-LOG2E)
            hi, mid, lo = (piece.astype(F32) for piece in _split3(f_key))
            s0 = spare[t % 2]
            bias = jnp.where(lane == s0, hi, jnp.where(lane == s0 + 1, mid, jnp.where(lane == s0 + 2, lo, 0.0)))
            kx_ref[t] = jnp.where(data[t % 2], k, bias).astype(BF16)
            vx_ref[t] = jnp.where(data[t % 2], v, jnp.where(lane == s0, 1.0, 0.0)).astype(BF16)

    f_rows = f_ref[pl.ds(pl.multiple_of(qi * tq, tq), tq), :]
    q_h, fq_h = [], []
    for t in range(n_heads):
        q = q_ref[:, pair_lanes(t)].astype(F32)
        s0 = spare[t % 2]
        ones = (lane >= s0) & (lane < s0 + 3)
        q_h.append(jnp.where(data[t % 2], q, jnp.where(ones, 1.0, 0.0)).astype(BF16))
        fq_h.append(jnp.sum(jnp.where(lane == n_heads * g + t, f_rows, 0.0), axis=1, keepdims=True) * LOG2E)
    causal = lax.broadcasted_iota(jnp.int32, (tq, tq), 1) <= lax.broadcasted_iota(jnp.int32, (tq, tq), 0)

    def step(j, carry, masked):
        off = pl.multiple_of(j * tq, tq)
        out = []
        for t in range(n_heads):
            m, acc = carry[t]
            s = lax.dot_general(q_h[t], kx_ref[t, pl.ds(off, tq), :], (((1,), (1,)), ((), ())),
                                preferred_element_type=F32)
            if masked:
                s = jnp.where(causal, s, NEG_INF)
            m_new = jnp.maximum(m, jnp.max(s, axis=1, keepdims=True) + fq_h[t])
            pexp = jnp.exp2(s + (fq_h[t] - m_new))
            acc = jnp.exp2(m - m_new) * acc + _dot(pexp.astype(BF16), vx_ref[t, pl.ds(off, tq), :])
            out.append((m_new, acc))
        return tuple(out)

    init = tuple((jnp.full((tq, 1), NEG_INF, F32), jnp.zeros((tq, LANES), F32)) for _ in range(n_heads))
    carry = lax.fori_loop(0, qi, lambda j, c: step(j, c, False), init)
    def diagonal(carry):
        half = tq // 2
        off = pl.multiple_of(qi * tq, tq)
        off_r = pl.multiple_of(off + half, half)
        contract = (((1,), (1,)), ((), ()))
        out = []
        for t in range(n_heads):
            m, acc = carry[t]
            s = lax.dot_general(q_h[t], kx_ref[t, pl.ds(off, half), :], contract, preferred_element_type=F32)
            s = jnp.where(causal[:, :half], s, NEG_INF)
            m_l = jnp.maximum(m, jnp.max(s, axis=1, keepdims=True) + fq_h[t])
            pexp = jnp.exp2(s + (fq_h[t] - m_l))
            acc = jnp.exp2(m - m_l) * acc + _dot(pexp.astype(BF16), vx_ref[t, pl.ds(off, half), :])
            fq_b, m_b, acc_b = fq_h[t][half:], m_l[half:], acc[half:]
            s = lax.dot_general(q_h[t][half:], kx_ref[t, pl.ds(off_r, half), :], contract,
                                preferred_element_type=F32)
            s = jnp.where(causal[:half, :half], s, NEG_INF)
            m_r = jnp.maximum(m_b, jnp.max(s, axis=1, keepdims=True) + fq_b)
            pexp = jnp.exp2(s + (fq_b - m_r))
            acc_b = jnp.exp2(m_b - m_r) * acc_b + _dot(pexp.astype(BF16), vx_ref[t, pl.ds(off_r, half), :])
            out.append(jnp.concatenate([acc[:half], acc_b], axis=0))
        return out

    accs = diagonal(carry)
    for t in range(0, n_heads, 2):
        out_a = accs[t] / accs[t][:, spare[0]:spare[0] + 1]
        out_b = accs[t + 1] / accs[t + 1][:, spare[1]:spare[1] + 1]
        o_ref[:, pair_lanes(t)] = jnp.where(first, out_a, out_b).astype(o_ref.dtype)


def _fox_attn(qkv, f_cum, *, seq, d_attn, tq=512, heads=4):
    t = qkv.shape[0]
    bsz = t // seq
    width = heads * HEAD_DIM
    n_groups = d_attn // width
    nq = seq // tq
    kern = functools.partial(_fox_attn_kernel, tq=tq)
    return pl.pallas_call(
        kern,
        grid=(bsz, n_groups, nq),
        in_specs=[pl.BlockSpec((tq, width), lambda b, g, i: (b * nq + i, g)),
                  pl.BlockSpec((seq, width), lambda b, g, i: (b, n_groups + g)),
                  pl.BlockSpec((seq, width), lambda b, g, i: (b, 2 * n_groups + g)),
                  pl.BlockSpec((seq, LANES), lambda b, g, i: (b, 0))],
        out_specs=pl.BlockSpec((tq, width), lambda b, g, i: (b * nq + i, g)),
        out_shape=jax.ShapeDtypeStruct((t, d_attn), BF16),
        scratch_shapes=[pltpu.VMEM((heads, seq, LANES), BF16), pltpu.VMEM((heads, seq, LANES), BF16)],
        compiler_params=_params(("arbitrary", "arbitrary", "arbitrary"), 48),
        name="fox_attn",
    )(qkv, qkv, qkv, f_cum)


def _conv_kernel(cur_ref, halo_ref, w_ref, cb_ref, lg_ref, lb_ref, o_ref, ybuf, cbuf, *, ts):
    qi = pl.program_id(1)
    c = cur_ref.shape[1]
    halo = halo_ref[...].astype(F32)
    ybuf[0:CONV_HALO, :] = jnp.where(qi > 0, halo, 0.0)
    ybuf[CONV_HALO:CONV_HALO + ts, :] = cur_ref[...].astype(F32)

    def lane_chunk(ci, _):
        lo = pl.multiple_of(ci * LANES, LANES)
        for r in range(ts // CONV_ROWS):
            acc = jnp.broadcast_to(cb_ref[:, pl.ds(lo, LANES)], (CONV_ROWS, LANES))
            for j in range(CONV_K):
                start = r * CONV_ROWS + CONV_HALO - (CONV_K - 1) + j
                acc = acc + w_ref[j:j + 1, pl.ds(lo, LANES)] * ybuf[start:start + CONV_ROWS, pl.ds(lo, LANES)]
            cbuf[r * CONV_ROWS:(r + 1) * CONV_ROWS, pl.ds(lo, LANES)] = acc
        return 0

    lax.fori_loop(0, c // LANES, lane_chunk, 0)
    y = cbuf[...]
    mu = jnp.mean(y, axis=-1, keepdims=True)
    yc = y - mu
    z = yc * lax.rsqrt(jnp.mean(yc * yc, axis=-1, keepdims=True) + EPS) * lg_ref[...] + lb_ref[...]
    o_ref[...] = _silu(z).astype(o_ref.dtype)


def _conv_branch(y, conv_w, conv_b, ln_g, ln_b, *, seq, ts=256):
    t, c = y.shape
    bsz = t // seq
    ns = seq // ts
    w = jnp.pad(conv_w.astype(F32), ((0, CONV_HALO - CONV_K), (0, 0)))
    kern = functools.partial(_conv_kernel, ts=ts)
    row = lambda a: a.astype(F32).reshape(1, c)
    return pl.pallas_call(
        kern,
        grid=(bsz, ns),
        in_specs=[pl.BlockSpec((ts, c), lambda b, i: (b * ns + i, 0)),
                  pl.BlockSpec((CONV_HALO, c),
                               lambda b, i: (jnp.maximum((b * ns + i) * (ts // CONV_HALO) - 1, 0), 0)),
                  pl.BlockSpec((CONV_HALO, c), lambda b, i: (0, 0)),
                  pl.BlockSpec((1, c), lambda b, i: (0, 0)),
                  pl.BlockSpec((1, c), lambda b, i: (0, 0)),
                  pl.BlockSpec((1, c), lambda b, i: (0, 0))],
        out_specs=pl.BlockSpec((ts, c), lambda b, i: (b * ns + i, 0)),
        out_shape=jax.ShapeDtypeStruct((t, c), BF16),
        scratch_shapes=[pltpu.VMEM((CONV_HALO + ts, c), F32), pltpu.VMEM((ts, c), F32)],
        compiler_params=_params(("arbitrary", "arbitrary"), 32),
        name="conv_branch",
    )(y, y, w, row(conv_b), row(ln_g), row(ln_b))


def _mix_out_kernel(z_ref, o_ref, sga_ref, sgb_ref, co_ref, ao_ref, wo_ref, x_ref, mod_ref, out_ref, acc_ref):
    j = pl.program_id(1)
    y_a = _dot(z_ref[...], co_ref[...])
    y_b = _dot(o_ref[...], ao_ref[...])
    merged = (sga_ref[...].astype(F32) * y_a + sgb_ref[...].astype(F32) * y_b).astype(BF16)

    @pl.when(j == 0)
    def _():
        acc_ref[...] = jnp.zeros_like(acc_ref)

    acc_ref[...] += _dot(merged, wo_ref[...])

    @pl.when(j == pl.num_programs(1) - 1)
    def _():
        out_ref[...] = x_ref[...] + mod_ref[2:3, :] * acc_ref[...]


def _mix_out(z, o, gates, conv_out, attn_out, w_out, layer, x, mod, *, seq, tm=512, tn=1024):
    t, d = x.shape
    c = z.shape[1]
    nj = d // tn
    tiles_per_batch = seq // tm
    return pl.pallas_call(
        _mix_out_kernel,
        grid=(t // tm, nj),
        in_specs=[pl.BlockSpec((tm, c), lambda i, j: (i, 0)),
                  pl.BlockSpec((tm, c), lambda i, j: (i, 0)),
                  pl.BlockSpec((tm, tn), lambda i, j: (i, j)),
                  pl.BlockSpec((tm, tn), lambda i, j: (i, nj + j)),
                  pl.BlockSpec((None, c, tn), lambda i, j: (layer, 0, j)),
                  pl.BlockSpec((None, c, tn), lambda i, j: (layer, 0, j)),
                  pl.BlockSpec((None, tn, d), lambda i, j: (layer, j, 0)),
                  pl.BlockSpec((tm, d), lambda i, j: (i, 0)),
                  pl.BlockSpec((None, 6, d), lambda i, j: (i // tiles_per_batch, 0, 0))],
        out_specs=pl.BlockSpec((tm, d), lambda i, j: (i, 0)),
        out_shape=jax.ShapeDtypeStruct((t, d), F32),
        scratch_shapes=[pltpu.VMEM((tm, d), F32)],
        compiler_params=_params(("arbitrary", "arbitrary"), 48),
        name="mix_out",
    )(z, o, gates, gates, conv_out, attn_out, w_out, x, mod)


def _ffn_kernel(x_ref, mod_ref, g_ref, w1_ref, w3_ref, w2_ref, out_ref, h_ref, acc_ref):
    j = pl.program_id(1)

    @pl.when(j == 0)
    def _():
        h_ref[...] = _norm_mod(x_ref[...], g_ref[...], mod_ref[3:4, :], mod_ref[4:5, :]).astype(BF16)
        acc_ref[...] = jnp.zeros_like(acc_ref)

    h = h_ref[...]
    hidden = (_silu(_dot(h, w1_ref[...])) * _dot(h, w3_ref[...])).astype(BF16)
    acc_ref[...] += _dot(hidden, w2_ref[...])

    @pl.when(j == pl.num_programs(1) - 1)
    def _():
        out_ref[...] = x_ref[...] + mod_ref[5:6, :] * acc_ref[...]


def _ffn(x, mod, g, w1, w3, w2, layer, *, seq, tm=512, tf=512):
    t, d = x.shape
    f = w1.shape[2]
    tiles_per_batch = seq // tm
    return pl.pallas_call(
        _ffn_kernel,
        grid=(t // tm, f // tf),
        in_specs=[pl.BlockSpec((tm, d), lambda i, j: (i, 0)),
                  pl.BlockSpec((None, 6, d), lambda i, j: (i // tiles_per_batch, 0, 0)),
                  pl.BlockSpec((1, d), lambda i, j: (0, 0)),
                  pl.BlockSpec((None, d, tf), lambda i, j: (layer, 0, j)),
                  pl.BlockSpec((None, d, tf), lambda i, j: (layer, 0, j)),
                  pl.BlockSpec((None, tf, d), lambda i, j: (layer, j, 0))],
        out_specs=pl.BlockSpec((tm, d), lambda i, j: (i, 0)),
        out_shape=jax.ShapeDtypeStruct((t, d), F32),
        scratch_shapes=[pltpu.VMEM((tm, d), BF16), pltpu.VMEM((tm, d), F32)],
        compiler_params=_params(("arbitrary", "arbitrary"), 48),
        name="ffn",
    )(x, mod, g, w1, w3, w2)


def _router_kernel(x_ref, mod_ref, g_ref, rw_ref, rb_ref, h_ref, idx_ref, wt_ref):
    h = _norm_mod(x_ref[...], g_ref[...], mod_ref[3:4, :], mod_ref[4:5, :])
    h_ref[...] = h
    logits = jnp.dot(h, rw_ref[...], preferred_element_type=F32, precision=lax.Precision.HIGHEST) + rb_ref[...]
    lane = lax.broadcasted_iota(jnp.int32, logits.shape, 1).astype(F32)
    lg = jnp.where(lane < N_EXPERTS, logits, -jnp.inf)
    v1 = jnp.max(lg, axis=1, keepdims=True)
    i1 = jnp.min(jnp.where(lg == v1, lane, float(LANES)), axis=1, keepdims=True)
    lg2 = jnp.where(lane == i1, -jnp.inf, lg)
    v2 = jnp.max(lg2, axis=1, keepdims=True)
    i2 = jnp.min(jnp.where(lg2 == v2, lane, float(LANES)), axis=1, keepdims=True)
    e2 = jnp.exp(v2 - v1)
    w1 = 1.0 / (1.0 + e2)
    w2 = e2 / (1.0 + e2)
    idx_ref[...] = jnp.where(lane == 0.0, i1, jnp.where(lane == 1.0, i2, 0.0)).astype(jnp.int32)
    wt_ref[...] = jnp.where(lane == 0.0, w1, jnp.where(lane == 1.0, w2, 0.0))


def _router(x, mod, g, router_w, router_b, *, seq, tm=512):
    t, d = x.shape
    tiles_per_batch = seq // tm
    rw = jnp.pad(router_w.astype(F32), ((0, 0), (0, LANES - N_EXPERTS)))
    rb = jnp.pad(router_b.astype(F32), (0, LANES - N_EXPERTS)).reshape(1, LANES)
    return pl.pallas_call(
        _router_kernel,
        grid=(t // tm,),
        in_specs=[pl.BlockSpec((tm, d), lambda i: (i, 0)),
                  pl.BlockSpec((None, 6, d), lambda i: (i // tiles_per_batch, 0, 0)),
                  pl.BlockSpec((1, d), lambda i: (0, 0)),
                  pl.BlockSpec((d, LANES), lambda i: (0, 0)),
                  pl.BlockSpec((1, LANES), lambda i: (0, 0))],
        out_specs=[pl.BlockSpec((tm, d), lambda i: (i, 0)),
                   pl.BlockSpec((tm, LANES), lambda i: (i, 0)),
                   pl.BlockSpec((tm, LANES), lambda i: (i, 0))],
        out_shape=[jax.ShapeDtypeStruct((t, d), F32),
                   jax.ShapeDtypeStruct((t, LANES), jnp.int32),
                   jax.ShapeDtypeStruct((t, LANES), F32)],
        compiler_params=_params(("arbitrary",), 40),
        name="router",
    )(x, mod, g, rw, rb)


def _dispatch_tables(idx, wts, *, tm, xrows):
    t = idx.shape[0]
    n_rows = TOP_K * t + N_EXPERTS * tm
    n_tiles = n_rows // tm
    flat_e = idx.reshape(-1)
    onehot = (flat_e[:, None] == jnp.arange(N_EXPERTS, dtype=jnp.int32)[None, :]).astype(jnp.int32)
    rank = jnp.sum((jnp.cumsum(onehot, axis=0) - onehot) * onehot, axis=1)
    counts = jnp.sum(onehot, axis=0)
    padded = ((counts + tm - 1) // tm) * tm
    ends = jnp.cumsum(padded)
    starts = ends - padded
    pos = starts[flat_e] + rank
    row_src = jnp.full((n_rows,), -1, jnp.int32).at[pos].set(jnp.arange(TOP_K * t, dtype=jnp.int32))
    valid = row_src >= 0
    src = jnp.maximum(row_src, 0)
    row_token = src // TOP_K
    row_dst = jnp.where(valid, (src % TOP_K) * t + row_token, -1)
    row_gate = jnp.where(valid, wts.reshape(-1)[src], 0.0).reshape(n_rows, 1)
    n_active = (ends[-1] // tm).astype(jnp.int32)
    tile_start = jnp.minimum(jnp.arange(n_tiles, dtype=jnp.int32), n_active - 1) * tm
    tile_expert = jnp.minimum(jnp.sum((tile_start[:, None] >= ends[None, :]).astype(jnp.int32), axis=1),
                              N_EXPERTS - 1)
    tile_rows = jnp.sum(valid.reshape(n_tiles, tm).astype(jnp.int32), axis=1)
    tile_token = jnp.pad(row_token.reshape(n_tiles, tm), ((0, 1), (0, xrows - tm))).reshape(-1)
    spare = TOP_K * t + jnp.arange(xrows, dtype=jnp.int32)
    dst = jnp.pad(row_dst.reshape(n_tiles, tm), ((0, 0), (0, xrows - tm)), constant_values=-1)
    dst = jnp.where(dst >= 0, dst, spare[None, :])
    prev_dst = jnp.concatenate([spare[None, :], dst], axis=0).reshape(-1)
    return tile_expert, n_active.reshape(1), tile_token, prev_dst, tile_rows, row_gate


def _moe_kernel(te_ref, na_ref, tok_ref, dst_ref, rows_ref, h_hbm, gate_ref, w1_ref, w3_ref, w2_ref, ys_hbm,
                xbuf, xbf, acc_ref, obuf, sem_in, sem_out, *, tm, sub, per_step):
    i = pl.program_id(0)
    j = pl.program_id(1)
    last_j = pl.num_programs(1) - 1
    n_active = na_ref[0]
    active = i < n_active
    xrows = xbuf.shape[0]

    def gather_row(r, tok):
        return pltpu.make_async_copy(h_hbm.at[pl.ds(tok, 1)], xbuf.at[pl.ds(r, 1)], sem_in)

    def scatter_row(r, dst):
        return pltpu.make_async_copy(obuf.at[pl.ds(r, 1)], ys_hbm.at[pl.ds(dst, 1)], sem_out)

    def wait_gathered():
        pltpu.make_async_copy(h_hbm.at[pl.ds(0, xrows)], xbuf, sem_in).wait()

    @pl.when(active & (j == 0))
    def _():
        @pl.when(i == 0)
        def _():
            obuf[...] = jnp.zeros_like(obuf)

            def issue(r, _):
                gather_row(r, tok_ref[r]).start()
                return 0

            lax.fori_loop(0, xrows, issue, 0, unroll=8)

        wait_gathered()
        xbf[...] = xbuf[0:tm, :].astype(BF16)

    @pl.when(active)
    def _():
        w1 = w1_ref[...].astype(BF16)
        w3 = w3_ref[...].astype(BF16)
        w2 = w2_ref[...].astype(BF16)
        n_valid = rows_ref[i]
        for s in range(tm // sub):
            rows = slice(s * sub, (s + 1) * sub)

            def compute(rows=rows, s=s):
                @pl.when(j == 0)
                def _():
                    acc_ref[rows, :] = jnp.zeros((sub, acc_ref.shape[1]), F32)

                x = xbf[rows, :]
                hidden = (_silu(_dot(x, w1)) * _dot(x, w3)).astype(BF16)
                acc_ref[rows, :] += _dot(hidden, w2)

                if s == 0:
                    for u in range(per_step):
                        r = j * per_step + u
                        gather_row(r, tok_ref[(i + 1) * xrows + r]).start()
                        scatter_row(r, dst_ref[i * xrows + r]).start()

                @pl.when(j == last_j)
                def _():
                    if s == 0:
                        pltpu.make_async_copy(obuf, ys_hbm.at[pl.ds(0, xrows)], sem_out).wait()
                    obuf[rows, :] = acc_ref[rows, :] * gate_ref[rows, :]

            if s == 0:
                compute()
            else:
                pl.when(n_valid > s * sub)(compute)

    @pl.when(active & (j == last_j) & (i == n_active - 1))
    def _():
        n_valid = rows_ref[i]
        base = (i + 1) * xrows

        def issue(r, _):
            scatter_row(r, dst_ref[base + r]).start()
            return 0

        def wait(r, _):
            scatter_row(0, 0).wait()
            return 0

        lax.fori_loop(0, n_valid, issue, 0)
        wait_gathered()
        lax.fori_loop(0, n_valid, wait, 0)


def _moe_rows_per_step(tm, f, tf):
    nf = f // tf
    return -(-tm // (8 * nf)) * 8


def _moe_experts(h, tables, w1, w3, w2, layer, *, tm, sub=512, tf=256):
    t, d = h.shape
    f = w1.shape[3]
    nf = f // tf
    assert nf >= 2 and tm % sub == 0
    per_step = _moe_rows_per_step(tm, f, tf)
    xrows = nf * per_step
    tile_expert, n_active, tile_token, prev_dst, tile_rows, row_gate = tables
    n_tiles = tile_expert.shape[0]
    assert tile_token.shape[0] == (n_tiles + 1) * xrows
    kern = functools.partial(_moe_kernel, tm=tm, sub=sub, per_step=per_step)

    def f_idx(i, j, na):
        return jnp.where(i < na[0], j, nf - 1)

    grid_spec = pltpu.PrefetchScalarGridSpec(
        num_scalar_prefetch=5,
        grid=(n_tiles, nf),
        in_specs=[pl.BlockSpec(memory_space=pl.ANY),
                  pl.BlockSpec((tm, 1), lambda i, j, te, na, *_: (i, 0)),
                  pl.BlockSpec((None, None, d, tf), lambda i, j, te, na, *_: (layer, te[i], 0, f_idx(i, j, na))),
                  pl.BlockSpec((None, None, d, tf), lambda i, j, te, na, *_: (layer, te[i], 0, f_idx(i, j, na))),
                  pl.BlockSpec((None, None, tf, d), lambda i, j, te, na, *_: (layer, te[i], f_idx(i, j, na), 0))],
        out_specs=pl.BlockSpec(memory_space=pl.ANY),
        scratch_shapes=[pltpu.VMEM((xrows, d), F32), pltpu.VMEM((tm, d), BF16), pltpu.VMEM((tm, d), F32),
                        pltpu.VMEM((xrows, d), F32), pltpu.SemaphoreType.DMA(()), pltpu.SemaphoreType.DMA(())],
    )
    return pl.pallas_call(
        kern,
        grid_spec=grid_spec,
        out_shape=jax.ShapeDtypeStruct((TOP_K * t + xrows, d), F32),
        compiler_params=_params(("arbitrary", "arbitrary"), 58),
        name="moe_experts",
    )(tile_expert, n_active, tile_token, prev_dst, tile_rows, h, row_gate, w1, w3, w2)


def _combine_kernel(x_ref, y0_ref, y1_ref, mod_ref, g_ref, out_ref, *, final):
    x = x_ref[...] + mod_ref[5:6, :] * (y0_ref[...] + y1_ref[...])
    if final:
        x = x * lax.rsqrt(jnp.mean(x * x, axis=-1, keepdims=True) + EPS) * g_ref[...]
    out_ref[...] = x


def _combine(x, ys, mod, final_g, *, seq, final, tm=512):
    t, d = x.shape
    nt = t // tm
    tiles_per_batch = seq // tm
    return pl.pallas_call(
        functools.partial(_combine_kernel, final=final),
        grid=(nt,),
        in_specs=[pl.BlockSpec((tm, d), lambda i: (i, 0)),
                  pl.BlockSpec((tm, d), lambda i: (i, 0)),
                  pl.BlockSpec((tm, d), lambda i: (nt + i, 0)),
                  pl.BlockSpec((None, 6, d), lambda i: (i // tiles_per_batch, 0, 0)),
                  pl.BlockSpec((1, d), lambda i: (0, 0))],
        out_specs=pl.BlockSpec((tm, d), lambda i: (i, 0)),
        out_shape=jax.ShapeDtypeStruct((t, d), F32),
        compiler_params=_params(("arbitrary",), 48),
        name="moe_combine",
    )(x, ys, ys, mod, final_g)


def _final_norm_kernel(x_ref, g_ref, o_ref):
    x = x_ref[...]
    o_ref[...] = x * lax.rsqrt(jnp.mean(x * x, axis=-1, keepdims=True) + EPS) * g_ref[...]


def _final_norm(x, g, tm=512):
    t, d = x.shape
    return pl.pallas_call(
        _final_norm_kernel,
        grid=(t // tm,),
        in_specs=[pl.BlockSpec((tm, d), lambda i: (i, 0)), pl.BlockSpec((1, d), lambda i: (0, 0))],
        out_specs=pl.BlockSpec((tm, d), lambda i: (i, 0)),
        out_shape=jax.ShapeDtypeStruct((t, d), F32),
        compiler_params=_params(("arbitrary",), 32),
        name="final_norm",
    )(x, g)


def kernel(x, c, ada_w, ada_b, norm_mix_g, norm_ffn_g, w_in, b_f, conv_w, conv_b, conv_ln_g, conv_ln_b,
           conv_out, attn_out, w_out, ffn_w1, ffn_w3, ffn_w2, router_w, router_b, moe_w1, moe_w3, moe_w2,
           final_norm_g):
    bsz, seq, d = x.shape
    depth = ada_w.shape[0]
    d_conv = conv_w.shape[2]
    d_attn = attn_out.shape[1]
    t = bsz * seq
    moe_tm, moe_tf = 1024, 256
    final_g = final_norm_g.reshape(1, d)
    conv_out_b, attn_out_b, w_out_b = (w.astype(BF16) for w in (conv_out, attn_out, w_out))
    ffn_w1_b, ffn_w3_b, ffn_w2_b = (w.astype(BF16) for w in (ffn_w1, ffn_w3, ffn_w2))

    mods = _adaln(c, ada_w, ada_b).reshape(depth, bsz, 6, d)
    xf = x.reshape(t, d)
    for l in range(depth):
        mod = mods[l]
        w_parts = _split_w_in(w_in[l], d_attn=d_attn, d_conv=d_conv)
        qkv, glu, gates, fg = _mix_in(xf, mod, norm_mix_g[l].reshape(1, d), w_parts, seq=seq)
        z = _conv_branch(glu, conv_w[l], conv_b[l], conv_ln_g[l], conv_ln_b[l], seq=seq)
        o = _fox_attn(qkv, _fox_gate(fg, b_f[l], seq=seq), seq=seq, d_attn=d_attn)
        xf = _mix_out(z, o, gates, conv_out_b, attn_out_b, w_out_b, l, xf, mod, seq=seq)
        g_ffn = norm_ffn_g[l].reshape(1, d)
        i = l // 2
        if l % 2 == 0:
            xf = _ffn(xf, mod, g_ffn, ffn_w1_b, ffn_w3_b, ffn_w2_b, i, seq=seq)
        else:
            h, idx, wts = _router(xf, mod, g_ffn, router_w[i], router_b[i], seq=seq)
            xrows = (moe_w1.shape[3] // moe_tf) * _moe_rows_per_step(moe_tm, moe_w1.shape[3], moe_tf)
            tables = _dispatch_tables(idx[:, :TOP_K], wts[:, :TOP_K], tm=moe_tm, xrows=xrows)
            ys = _moe_experts(h, tables, moe_w1, moe_w3, moe_w2, i, tm=moe_tm, tf=moe_tf)
            xf = _combine(xf, ys, mod, final_g, seq=seq, final=(l == depth - 1))
    if depth % 2 == 1:
        xf = _final_norm(xf, final_g)
    return xf.reshape(bsz, seq, d)
```
